```python
import math
import jax
import jax.numpy as jnp
from jax import lax
import numpy as np

D_MODEL = 1024
BATCH = 4
SEQ = 8192
DEPTH = 4

N_MIXERS = 3
N_FOX_LAYERS = (DEPTH + 2) // 3
N_GLA_LAYERS = (DEPTH + 1) // 3
N_GDN_LAYERS = DEPTH // 3
RMS_EPS = 1e-6

FOX_HEADS = 8
FOX_HEAD_DIM = D_MODEL // FOX_HEADS
FOX_WIDTH = FOX_HEADS * FOX_HEAD_DIM
FOX_BLOCK = 128
FOX_FGATE_BIAS = 3.0
FOX_IN_DIM = 4 * FOX_WIDTH + FOX_HEADS

GLA_HEADS = 4
GLA_KEY_DIM = D_MODEL // 2 // GLA_HEADS
GLA_VAL_DIM = D_MODEL // GLA_HEADS
GLA_WIDTH = GLA_HEADS * GLA_VAL_DIM
GLA_RANK = 16
GLA_TAU = 16.0
GLA_CHUNK = 64
GLA_IN_DIM = 2 * GLA_HEADS * GLA_KEY_DIM + 2 * GLA_WIDTH + GLA_RANK

GDN_QK_HEADS = 4
GDN_V_HEADS = 8
GDN_HEAD_DIM = D_MODEL // GDN_V_HEADS
GDN_WIDTH = GDN_V_HEADS * GDN_HEAD_DIM
GDN_CONV = 4
GDN_CHUNK = 64
GDN_CONV_DIM = 2 * GDN_QK_HEADS * GDN_HEAD_DIM + GDN_WIDTH
GDN_IN_DIM = GDN_CONV_DIM + GDN_WIDTH + 2 * GDN_V_HEADS

kernel_name = 'hybrid_fox_gla_gdn_trunk'


def rms_norm(x, w):
    xf = x.astype(jnp.float32)
    y = xf * lax.rsqrt(jnp.mean(xf * xf, axis=-1, keepdims=True) + RMS_EPS)
    return (y * w.astype(jnp.float32)).astype(x.dtype)


def l2_norm(x):
    xf = x.astype(jnp.float32)
    return xf * lax.rsqrt(jnp.sum(xf * xf, axis=-1, keepdims=True) + RMS_EPS)


def to_chunks(t, chunk):
    b, s, h = t.shape[:3]
    t = t.reshape((b, s // chunk, chunk, h) + t.shape[3:])
    return jnp.moveaxis(t, 3, 1)


def from_chunks(t):
    b, h, nc, l, d = t.shape
    return jnp.moveaxis(t, 1, 3).reshape(b, nc * l, h, d)


def causal_conv(u, w):
    k, c = w.shape
    return lax.conv_general_dilated(u, w[:, None, :].astype(u.dtype), window_strides=(1,), padding=[(k - 1, 0)], dimension_numbers=('NWC', 'WIO', 'NWC'), feature_group_count=c)


def fox_mixer(xn, w_in, b_f, q_gain, k_gain, w_out):
    bsz, s, _ = xn.shape
    h, dh = FOX_HEADS, FOX_HEAD_DIM
    q, k, v, z, f_logit = jnp.split(xn @ w_in, [FOX_WIDTH, 2 * FOX_WIDTH, 3 * FOX_WIDTH, 4 * FOX_WIDTH], axis=-1)
    q = rms_norm(q.reshape(bsz, s, h, dh), q_gain).transpose(0, 2, 1, 3)
    k = rms_norm(k.reshape(bsz, s, h, dh), k_gain).transpose(0, 2, 1, 3)
    v = v.reshape(bsz, s, h, dh).transpose(0, 2, 1, 3)
    log_f = jax.nn.log_sigmoid((f_logit + b_f).astype(jnp.float32))
    cum = jnp.cumsum(log_f, axis=1).transpose(0, 2, 1)
    nb = s // FOX_BLOCK
    q_blocks = q.reshape(bsz, h, nb, FOX_BLOCK, dh).transpose(2, 0, 1, 3, 4)
    c_blocks = cum.reshape(bsz, h, nb, FOX_BLOCK).transpose(2, 0, 1, 3)
    pos_blocks = jnp.arange(s).reshape(nb, FOX_BLOCK)
    key_pos = jnp.arange(s)
    scale = dh ** -0.5

    def attend_block(blk):
        qb, cb, pb = blk
        logits = jnp.einsum('bhqd,bhkd->bhqk', qb, k).astype(jnp.float32) * scale + (cb[..., :, None] - cum[..., None, :])
        logits = jnp.where(pb[:, None] >= key_pos[None, :], logits, -jnp.inf)
        p = jax.nn.softmax(logits, axis=-1).astype(v.dtype)
        return jnp.einsum('bhqk,bhkd->bhqd', p, v)

    o = lax.map(attend_block, (q_blocks, c_blocks, pos_blocks))
    o = o.transpose(1, 0, 3, 2, 4).reshape(bsz, s, FOX_WIDTH)
    return (o * jax.nn.silu(z)) @ w_out


def gla_mixer(xn, w_in, w_gate_up, b_gate, o_gain, w_out):
    bsz, s, _ = xn.shape
    h, dk, dv, L = GLA_HEADS, GLA_KEY_DIM, GLA_VAL_DIM, GLA_CHUNK
    f32 = jnp.float32
    qk_w = h * dk
    q, k, v, z, g_low = jnp.split(xn @ w_in, [qk_w, 2 * qk_w, 2 * qk_w + GLA_WIDTH, 2 * qk_w + 2 * GLA_WIDTH], axis=-1)
    log_a = jax.nn.log_sigmoid((g_low @ w_gate_up + b_gate).astype(f32)) / GLA_TAU
    qc = to_chunks(q.reshape(bsz, s, h, dk).astype(f32) * dk ** -0.5, L)
    kc = to_chunks(k.reshape(bsz, s, h, dk).astype(f32), L)
    vc = to_chunks(v.reshape(bsz, s, h, dv).astype(f32), L)
    ac = to_chunks(log_a.reshape(bsz, s, h, dk), L)
    bcum = jnp.cumsum(ac, axis=3)
    b_last = bcum[..., -1:, :]
    q_dec = qc * jnp.exp(bcum)
    k_inv = kc * jnp.exp(-bcum)
    k_dec = kc * jnp.exp(b_last - bcum)
    causal = jnp.tril(jnp.ones((L, L), dtype=bool))
    attn = jnp.where(causal, jnp.einsum('bhnld,bhnmd->bhnlm', q_dec, k_inv), 0.0)
    o_intra = jnp.einsum('bhnlm,bhnmv->bhnlv', attn, vc)

    def step(state, inp):
        qd, kd, vv, dec = inp
        o = jnp.einsum('bhld,bhdv->bhlv', qd, state)
        state = state * dec[..., :, None] + jnp.einsum('bhld,bhlv->bhdv', kd, vv)
        return state, o

    xs = (jnp.moveaxis(q_dec, 2, 0), jnp.moveaxis(k_dec, 2, 0), jnp.moveaxis(vc, 2, 0), jnp.moveaxis(jnp.exp(b_last[..., 0, :]), 2, 0))
    state0 = jnp.zeros((bsz, h, dk, dv), f32)
    _, o_inter = lax.scan(step, state0, xs)
    o = o_intra + jnp.moveaxis(o_inter, 0, 2)
    o = rms_norm(from_chunks(o), o_gain).reshape(bsz, s, GLA_WIDTH).astype(xn.dtype)
    return (o * jax.nn.silu(z)) @ w_out


def gdn_mixer(xn, w_in, conv_w, a_log, dt_bias, o_gain, w_out):
    bsz, s, _ = xn.shape
    hq, hv, d, L = GDN_QK_HEADS, GDN_V_HEADS, GDN_HEAD_DIM, GDN_CHUNK
    f32 = jnp.float32
    qkv, z, a, b = jnp.split(xn @ w_in, [GDN_CONV_DIM, GDN_CONV_DIM + GDN_WIDTH, GDN_CONV_DIM + GDN_WIDTH + hv], axis=-1)
    qkv = jax.nn.silu(causal_conv(qkv, conv_w))
    q, k, v = jnp.split(qkv, [hq * d, 2 * hq * d], axis=-1)
    rep = hv // hq
    q = jnp.repeat(l2_norm(q.reshape(bsz, s, hq, d)), rep, axis=2) * d ** -0.5
    k = jnp.repeat(l2_norm(k.reshape(bsz, s, hq, d)), rep, axis=2)
    v = v.reshape(bsz, s, hv, d).astype(f32)
    beta = jax.nn.sigmoid(b.astype(f32))
    g = -jnp.exp(a_log) * jax.nn.softplus(a.astype(f32) + dt_bias)
    qc, kc, vc = to_chunks(q, L), to_chunks(k, L), to_chunks(v, L)
    bc = to_chunks(beta, L)
    gc = jnp.cumsum(to_chunks(g, L), axis=-1)
    causal = jnp.tril(jnp.ones((L, L), dtype=bool))
    strict = jnp.tril(jnp.ones((L, L), dtype=bool), k=-1)
    diff = gc[..., :, None] - gc[..., None, :]
    decay = jnp.where(causal, jnp.exp(jnp.where(causal, diff, 0.0)), 0.0)
    kb = kc * bc[..., None]
    vb = vc * bc[..., None]
    tri = jnp.where(strict, jnp.einsum('bhnld,bhnmd->bhnlm', kb, kc) * decay, 0.0) + jnp.eye(L, dtype=f32)
    u = lax.linalg.triangular_solve(tri, vb, left_side=True, lower=True)
    w = lax.linalg.triangular_solve(tri, kb * jnp.exp(gc)[..., None], left_side=True, lower=True)
    qk = jnp.einsum('bhnld,bhnmd->bhnlm', qc, kc) * decay
    q_dec = qc * jnp.exp(gc)[..., None]
    g_last = gc[..., -1]
    k_dec = kc * jnp.exp(g_last[..., None] - gc)[..., None]

    def step(state, inp):
        uu, ww, qkk, qd, kd, dec = inp
        v_new = uu - jnp.einsum('bhld,bhdv->bhlv', ww, state)
        o = jnp.einsum('bhld,bhdv->bhlv', qd, state) + jnp.einsum('bhlm,bhmv->bhlv', qkk, v_new)
        state = state * dec[..., None, None] + jnp.einsum('bhld,bhlv->bhdv', kd, v_new)
        return state, o

    xs = tuple(jnp.moveaxis(t, 2, 0) for t in (u, w, qk, q_dec, k_dec, jnp.exp(g_last)))
    state0 = jnp.zeros((bsz, hv, d, d), f32)
    _, o_inter = lax.scan(step, state0, xs)
    o = from_chunks(jnp.moveaxis(o_inter, 0, 2))
    o = rms_norm(o, o_gain).reshape(bsz, s, GDN_WIDTH).astype(xn.dtype)
    return (o * jax.nn.silu(z)) @ w_out


def setup_inputs(seed: int = 0) -> dict:
    key = jax.random.key(seed)
    ks = jax.random.split(key, 20)
    f32 = jnp.float32

    def nrm(k, shape, scale):
        return jax.random.normal(k, shape, f32) * scale

    x = nrm(ks[0], (BATCH, SEQ, D_MODEL), 1.0)
    norm_w = 1.0 + nrm(ks[1], (DEPTH, D_MODEL), 0.02)
    fox_w_in = nrm(ks[2], (N_FOX_LAYERS, D_MODEL, FOX_IN_DIM), D_MODEL ** -0.5)
    fox_b_f = FOX_FGATE_BIAS + nrm(ks[3], (N_FOX_LAYERS, FOX_HEADS), 0.5)
    fox_q_gain = 1.0 + nrm(ks[4], (N_FOX_LAYERS, FOX_HEAD_DIM), 0.02)
    fox_k_gain = 1.0 + nrm(ks[5], (N_FOX_LAYERS, FOX_HEAD_DIM), 0.02)
    fox_w_out = nrm(ks[6], (N_FOX_LAYERS, FOX_WIDTH, D_MODEL), FOX_WIDTH ** -0.5)
    gla_w_in = nrm(ks[7], (N_GLA_LAYERS, D_MODEL, GLA_IN_DIM), D_MODEL ** -0.5)
    gla_w_gate_up = nrm(ks[8], (N_GLA_LAYERS, GLA_RANK, GLA_HEADS * GLA_KEY_DIM), GLA_RANK ** -0.5)
    gla_b_gate = nrm(ks[9], (N_GLA_LAYERS, GLA_HEADS * GLA_KEY_DIM), 0.1)
    gla_o_gain = 1.0 + nrm(ks[10], (N_GLA_LAYERS, GLA_VAL_DIM), 0.02)
    gla_w_out = nrm(ks[11], (N_GLA_LAYERS, GLA_WIDTH, D_MODEL), GLA_WIDTH ** -0.5)
    gdn_w_in = nrm(ks[12], (N_GDN_LAYERS, D_MODEL, GDN_IN_DIM), D_MODEL ** -0.5)
    gdn_conv_w = nrm(ks[13], (N_GDN_LAYERS, GDN_CONV, GDN_CONV_DIM), GDN_CONV ** -0.5)
    gdn_a_log = jnp.log(jax.random.uniform(ks[14], (N_GDN_LAYERS, GDN_V_HEADS), f32, minval=1.0, maxval=16.0))
    dt = jnp.exp(jax.random.uniform(ks[15], (N_GDN_LAYERS, GDN_V_HEADS), f32, minval=math.log(1e-3), maxval=math.log(1e-1)))
    gdn_dt_bias = dt + jnp.log(-jnp.expm1(-dt))
    gdn_o_gain = 1.0 + nrm(ks[16], (N_GDN_LAYERS, GDN_HEAD_DIM), 0.02)
    gdn_w_out = nrm(ks[17], (N_GDN_LAYERS, GDN_WIDTH, D_MODEL), GDN_WIDTH ** -0.5)
    return {'x': x, 'norm_w': norm_w,
            'fox_w_in': fox_w_in, 'fox_b_f': fox_b_f, 'fox_q_gain': fox_q_gain, 'fox_k_gain': fox_k_gain, 'fox_w_out': fox_w_out,
            'gla_w_in': gla_w_in, 'gla_w_gate_up': gla_w_gate_up, 'gla_b_gate': gla_b_gate, 'gla_o_gain': gla_o_gain, 'gla_w_out': gla_w_out,
            'gdn_w_in': gdn_w_in, 'gdn_conv_w': gdn_conv_w, 'gdn_a_log': gdn_a_log, 'gdn_dt_bias': gdn_dt_bias, 'gdn_o_gain': gdn_o_gain, 'gdn_w_out': gdn_w_out}


def reference(x, norm_w, fox_w_in, fox_b_f, fox_q_gain, fox_k_gain, fox_w_out, gla_w_in, gla_w_gate_up, gla_b_gate, gla_o_gain, gla_w_out, gdn_w_in, gdn_conv_w, gdn_a_log, gdn_dt_bias, gdn_o_gain, gdn_w_out):
    for layer in range(DEPTH):
        xn = rms_norm(x, norm_w[layer])
        kind, idx = layer % N_MIXERS, layer // N_MIXERS
        if kind == 0:
            y = fox_mixer(xn, fox_w_in[idx], fox_b_f[idx], fox_q_gain[idx], fox_k_gain[idx], fox_w_out[idx])
        elif kind == 1:
            y = gla_mixer(xn, gla_w_in[idx], gla_w_gate_up[idx], gla_b_gate[idx], gla_o_gain[idx], gla_w_out[idx])
        else:
            y = gdn_mixer(xn, gdn_w_in[idx], gdn_conv_w[idx], gdn_a_log[idx], gdn_dt_bias[idx], gdn_o_gain[idx], gdn_w_out[idx])
        x = x + y.astype(x.dtype)
    return x
```

```python
import functools
import math

import jax
import jax.numpy as jnp
from jax import lax
from jax.experimental import pallas as pl
from jax.experimental.pallas import tpu as pltpu

F32 = jnp.float32
BF16 = jnp.bfloat16

D_MODEL = 1024
DEPTH = 4
N_MIXERS = 3
RMS_EPS = 1e-6
LANES = 128

FOX_HEADS = 8
FOX_HEAD_DIM = 128
FOX_WIDTH = 1024

GLA_HEADS = 4
GLA_KEY_DIM = 128
GLA_VAL_DIM = 256
GLA_WIDTH = 1024
GLA_RANK = 16
GLA_TAU = 16.0
GLA_CHUNK = 64

GDN_QK_HEADS = 4
GDN_V_HEADS = 8
GDN_HEAD_DIM = 128
GDN_WIDTH = 1024
GDN_CONV = 4
GDN_CHUNK = 64
GDN_CONV_DIM = 2048

VMEM_LIMIT_BYTES = 56 * 1024 * 1024

HIGHEST = lax.Precision.HIGHEST


def _params(*semantics):
    return pltpu.CompilerParams(dimension_semantics=semantics, vmem_limit_bytes=VMEM_LIMIT_BYTES)


def _dot(a, b, precision=None):
    return jnp.dot(a, b, preferred_element_type=F32, precision=precision)


def _dot_nt(a, b):
    return lax.dot_general(a, b, (((1,), (1,)), ((), ())), preferred_element_type=F32)


def _sigmoid(x):
    return 1.0 / (1.0 + jnp.exp(-x))


def _silu(x):
    return x * _sigmoid(x)


def _softplus(x):
    return jnp.maximum(x, 0.0) + jnp.log(1.0 + jnp.exp(-jnp.abs(x)))


def _log_sigmoid(x):
    return -_softplus(-x)


def _chunk_cumsum(x, chunk):
    row = lax.broadcasted_iota(jnp.int32, x.shape, 0) & (chunk - 1)
    shift = 1
    while shift < chunk:
        x = x + jnp.where(row >= shift, pltpu.roll(x, shift, axis=0), 0.0)
        shift *= 2
    return x


def _inproj_kernel(x_ref, nw_ref, w_ref, we_ref, gain_ref, o_ref, oe_ref, *, tn, n_norm_groups):
    x = x_ref[...]
    ms = jnp.mean(x * x, axis=-1, keepdims=True)
    xn = (x * lax.rsqrt(ms + RMS_EPS) * nw_ref[...]).astype(BF16)
    oe_ref[...] = _dot(xn, we_ref[...])
    n = o_ref.shape[1]
    for j in range(n // tn):
        acc = _dot(xn, w_ref[:, j * tn:(j + 1) * tn])
        for g in range(tn // LANES):
            col = j * tn + g * LANES
            seg = acc[:, g * LANES:(g + 1) * LANES]
            if col // LANES < n_norm_groups:
                msq = jnp.mean(seg * seg, axis=-1, keepdims=True)
                seg = seg * lax.rsqrt(msq + RMS_EPS) * gain_ref[:, col:col + LANES]
            o_ref[:, col:col + LANES] = seg.astype(o_ref.dtype)


def _inproj(x2d, norm_w, w_main, w_ext, gain, n_norm_groups, out_dtype, name):
    t, d = x2d.shape
    n = w_main.shape[1]
    tm, tn = 512, 512
    kern = functools.partial(_inproj_kernel, tn=tn, n_norm_groups=n_norm_groups)
    return pl.pallas_call(
        kern,
        grid=(t // tm,),
        in_specs=[
            pl.BlockSpec((tm, d), lambda i: (i, 0)),
            pl.BlockSpec((1, d), lambda i: (0, 0)),
            pl.BlockSpec((d, n), lambda i: (0, 0)),
            pl.BlockSpec((d, LANES), lambda i: (0, 0)),
            pl.BlockSpec((1, gain.shape[1]), lambda i: (0, 0)),
        ],
        out_specs=[
            pl.BlockSpec((tm, n), lambda i: (i, 0)),
            pl.BlockSpec((tm, LANES), lambda i: (i, 0)),
        ],
        out_shape=[
            jax.ShapeDtypeStruct((t, n), out_dtype),
            jax.ShapeDtypeStruct((t, LANES), F32),
        ],
        compiler_params=_params("parallel"),
        name=name,
    )(x2d, norm_w.reshape(1, d), w_main, w_ext, gain)


def _split_w_in(w_in, n_main):
    d, n = w_in.shape
    w_main = w_in[:, :n_main].astype(BF16)
    w_ext = jnp.zeros((d, LANES), F32).at[:, :n - n_main].set(w_in[:, n_main:]).astype(BF16)
    return w_main, w_ext


def _outproj_kernel(g_ref, w_ref, x_ref, o_ref):
    o_ref[...] = x_ref[...] + _dot(g_ref[...], w_ref[...])


def _outproj(g2d, w_out, x2d, name):
    t, d = x2d.shape
    k = g2d.shape[1]
    tm = 512
    return pl.pallas_call(
        _outproj_kernel,
        grid=(t // tm,),
        in_specs=[
            pl.BlockSpec((tm, k), lambda i: (i, 0)),
            pl.BlockSpec((k, d), lambda i: (0, 0)),
            pl.BlockSpec((tm, d), lambda i: (i, 0)),
        ],
        out_specs=pl.BlockSpec((tm, d), lambda i: (i, 0)),
        out_shape=jax.ShapeDtypeStruct((t, d), F32),
        compiler_params=_params("parallel"),
        name=name,
    )(g2d, w_out.astype(BF16), x2d)


def _fox_cumsum_kernel(f_ref, b_ref, tri_ref, ccol_ref, crow_ref, carry_ref):
    @pl.when(pl.program_id(1) == 0)
    def _():
        carry_ref[...] = jnp.zeros_like(carry_ref)

    log_f = _log_sigmoid(f_ref[0] + b_ref[...])
    cs = _dot(tri_ref[...], log_f, precision=HIGHEST) + carry_ref[...]
    ts = cs.shape[0]
    carry_ref[...] = cs[ts - 1:ts, :]
    ccol_ref[0] = cs
    crow_ref[0] = cs.T[:FOX_HEADS, :]


def _fox_cumsum(ext, b_f):
    b, s, _ = ext.shape
    ts = 256
    bias = jnp.zeros((1, LANES), F32).at[0, :FOX_HEADS].set(b_f)
    tri = jnp.tril(jnp.ones((ts, ts), F32))
    return pl.pallas_call(
        _fox_cumsum_kernel,
        grid=(b, s // ts),
        in_specs=[
            pl.BlockSpec((1, ts, LANES), lambda bi, i: (bi, i, 0)),
            pl.BlockSpec((1, LANES), lambda bi, i: (0, 0)),
            pl.BlockSpec((ts, ts), lambda bi, i: (0, 0)),
        ],
        out_specs=[
            pl.BlockSpec((1, ts, LANES), lambda bi, i: (bi, i, 0)),
            pl.BlockSpec((1, FOX_HEADS, ts), lambda bi, i: (bi, 0, i)),
        ],
        out_shape=[
            jax.ShapeDtypeStruct((b, s, LANES), F32),
            jax.ShapeDtypeStruct((b, FOX_HEADS, s), F32),
        ],
        scratch_shapes=[pltpu.VMEM((1, LANES), F32)],
        compiler_params=_params("parallel", "arbitrary"),
        name="fox_cumsum",
    )(ext, bias, tri)


def _fox_attn_kernel(q_ref, k_ref, v_ref, z_ref, ccol_ref, crow_ref, o_ref, *, tq):
    h = pl.program_id(1)
    i = pl.program_id(2)
    q = q_ref[0]
    lane = lax.broadcasted_iota(jnp.int32, (tq, LANES), 1)
    cq = jnp.sum(jnp.where(lane == h, ccol_ref[0], 0.0), axis=-1, keepdims=True)

    def attend(kb, carry, diagonal):
        m, l, acc = carry
        start = pl.multiple_of(kb * tq, tq)
        k = k_ref[0, pl.ds(start, tq), :]
        v = v_ref[0, pl.ds(start, tq), :]
        ck = crow_ref[0, :, pl.ds(start, tq)]
        t = _dot_nt(q, k) - ck
        if diagonal:
            row = lax.broadcasted_iota(jnp.int32, (tq, tq), 0)
            col = lax.broadcasted_iota(jnp.int32, (tq, tq), 1)
            t = jnp.where(row >= col, t, -jnp.inf)
        m_new = jnp.maximum(m, jnp.max(t, axis=-1, keepdims=True) + cq)
        alpha = jnp.exp(m - m_new)
        p = jnp.exp(t + (cq - m_new))
        l = alpha * l + jnp.sum(p, axis=-1, keepdims=True)
        acc = alpha * acc + _dot(p.astype(BF16), v)
        return m_new, l, acc

    init = (jnp.full((tq, 1), -jnp.inf, F32), jnp.zeros((tq, 1), F32), jnp.zeros((tq, LANES), F32))
    carry = lax.fori_loop(0, i, lambda kb, c: attend(kb, c, False), init)
    _, l, acc = attend(i, carry, True)
    z = z_ref[0].astype(F32)
    o_ref[0] = (acc / l * _silu(z)).astype(o_ref.dtype)


def _fox_attn(y, ccol, crow):
    b, s, _ = y.shape
    h = FOX_HEADS
    tq = 256
    crow3 = crow.reshape(b * h, 1, s)
    kern = functools.partial(_fox_attn_kernel, tq=tq)
    return pl.pallas_call(
        kern,
        grid=(b, h, s // tq),
        in_specs=[
            pl.BlockSpec((1, tq, LANES), lambda bi, hi, i: (bi, i, hi)),
            pl.BlockSpec((1, s, LANES), lambda bi, hi, i: (bi, 0, h + hi)),
            pl.BlockSpec((1, s, LANES), lambda bi, hi, i: (bi, 0, 2 * h + hi)),
            pl.BlockSpec((1, tq, LANES), lambda bi, hi, i: (bi, i, 3 * h + hi)),
            pl.BlockSpec((1, tq, LANES), lambda bi, hi, i: (bi, i, 0)),
            pl.BlockSpec((1, 1, s), lambda bi, hi, i: (bi * h + hi, 0, 0)),
        ],
        out_specs=pl.BlockSpec((1, tq, LANES), lambda bi, hi, i: (bi, i, hi)),
        out_shape=jax.ShapeDtypeStruct((b, s, FOX_WIDTH), BF16),
        compiler_params=_params("parallel", "parallel", "arbitrary"),
        name="fox_attn",
    )(y, y, y, y, ccol, crow3)


def _fox_layer(x2d, bsz, norm_w, w_in, b_f, q_gain, k_gain, w_out, tag):
    t, d = x2d.shape
    s = t // bsz
    w_main, w_ext = _split_w_in(w_in, 4 * FOX_WIDTH)
    scale = FOX_HEAD_DIM ** -0.5
    gain = jnp.concatenate([jnp.tile(q_gain * scale, FOX_HEADS), jnp.tile(k_gain, FOX_HEADS)]).reshape(1, -1)
    y, ext = _inproj(x2d, norm_w, w_main, w_ext, gain, 2 * FOX_HEADS, BF16, "fox_inproj" + tag)
    ccol, crow = _fox_cumsum(ext.reshape(bsz, s, LANES), b_f)
    g = _fox_attn(y.reshape(bsz, s, -1), ccol, crow)
    return _outproj(g.reshape(t, FOX_WIDTH), w_out, x2d, "fox_outproj" + tag)


def _gla_kernel(q_ref, k_ref, v_ref, z_ref, ext_ref, wup_ref, bg_ref, gain_ref, o_ref, state_ref, *, rows):
    chunk = GLA_CHUNK

    @pl.when(pl.program_id(2) == 0)
    def _():
        state_ref[...] = jnp.zeros_like(state_ref)

    gate = _dot(ext_ref[0].astype(BF16), wup_ref[...]) + bg_ref[...]
    log_a = _log_sigmoid(gate) * (1.0 / GLA_TAU)
    bcum = _chunk_cumsum(log_a, chunk)
    q = q_ref[0] * (GLA_KEY_DIM ** -0.5)
    k = k_ref[0]
    q_dec = (q * jnp.exp(bcum)).astype(BF16)
    k_inv = (k * jnp.exp(-bcum)).astype(BF16)
    row = lax.broadcasted_iota(jnp.int32, (chunk, chunk), 0)
    col = lax.broadcasted_iota(jnp.int32, (chunk, chunk), 1)
    causal = row >= col
    for c in range(rows // chunk):
        sl = slice(c * chunk, (c + 1) * chunk)
        b_last = bcum[(c + 1) * chunk - 1:(c + 1) * chunk, :]
        k_dec = (k[sl] * jnp.exp(b_last - bcum[sl])).astype(BF16)
        v = v_ref[0, sl, :]
        attn = jnp.where(causal, _dot_nt(q_dec[sl], k_inv[sl]), 0.0)
        state_t = state_ref[...]
        o = _dot(attn.astype(BF16), v.astype(BF16)) + _dot_nt(q_dec[sl], state_t.astype(BF16))
        state_ref[...] = state_t * jnp.exp(b_last) + _dot(v.T.astype(BF16), k_dec)
        ms = jnp.mean(o * o, axis=-1, keepdims=True)
        on = o * lax.rsqrt(ms + RMS_EPS) * gain_ref[...]
        o_ref[0, sl, :] = (on * _silu(z_ref[0, sl, :])).astype(o_ref.dtype)


def _gla_layer(x2d, bsz, norm_w, w_in, w_gate_up, b_gate, o_gain, w_out):
    t, d = x2d.shape
    s = t // bsz
    h, dk, dv = GLA_HEADS, GLA_KEY_DIM, GLA_VAL_DIM
    n_main = 2 * h * dk + 2 * GLA_WIDTH
    w_main, w_ext = _split_w_in(w_in, n_main)
    y, ext = _inproj(x2d, norm_w, w_main, w_ext, jnp.ones((1, LANES), F32), 0, F32, "gla_inproj")
    y = y.reshape(bsz, s, n_main)
    ext = ext.reshape(bsz, s, LANES)
    wup = jnp.zeros((LANES, h * dk), F32).at[:GLA_RANK].set(w_gate_up).astype(BF16)
    rows = 512
    kern = functools.partial(_gla_kernel, rows=rows)
    k_off = h * dk // dk
    v_off = 2 * h * dk // dv
    z_off = (2 * h * dk + GLA_WIDTH) // dv
    g = pl.pallas_call(
        kern,
        grid=(bsz, h, s // rows),
        in_specs=[
            pl.BlockSpec((1, rows, dk), lambda bi, hi, i: (bi, i, hi)),
            pl.BlockSpec((1, rows, dk), lambda bi, hi, i: (bi, i, k_off + hi)),
            pl.BlockSpec((1, rows, dv), lambda bi, hi, i: (bi, i, v_off + hi)),
            pl.BlockSpec((1, rows, dv), lambda bi, hi, i: (bi, i, z_off + hi)),
            pl.BlockSpec((1, rows, LANES), lambda bi, hi, i: (bi, i, 0)),
            pl.BlockSpec((LANES, dk), lambda bi, hi, i: (0, hi)),
            pl.BlockSpec((1, dk), lambda bi, hi, i: (0, hi)),
            pl.BlockSpec((1, dv), lambda bi, hi, i: (0, 0)),
        ],
        out_specs=pl.BlockSpec((1, rows, dv), lambda bi, hi, i: (bi, i, hi)),
        out_shape=jax.ShapeDtypeStruct((bsz, s, GLA_WIDTH), BF16),
        scratch_shapes=[pltpu.VMEM((dv, dk), F32)],
        compiler_params=_params("parallel", "parallel", "arbitrary"),
        name="gla_chunk",
    )(y, y, y, y, ext, wup, b_gate.reshape(1, -1), o_gain.reshape(1, -1))
    return _outproj(g.reshape(t, GLA_WIDTH), w_out, x2d, "gla_outproj")


def _gdn_conv_kernel(u_ref, tail_ref, w_ref, o_ref, ext_ref, *, ts):
    pad = 8
    first = pl.program_id(1) == 0
    ext_ref[0:pad, :] = jnp.where(first, 0.0, tail_ref[0])
    ext_ref[pad:pad + ts, :] = u_ref[0]
    acc = jnp.zeros(u_ref.shape[1:], F32)
    for tap in range(GDN_CONV):
        off = pad - (GDN_CONV - 1) + tap
        acc = acc + ext_ref[off:off + ts, :] * w_ref[tap:tap + 1, :]
    y = _silu(acc)
    n_qk = 2 * GDN_QK_HEADS
    for g in range(GDN_CONV_DIM // LANES):
        seg = y[:, g * LANES:(g + 1) * LANES]
        if g < n_qk:
            ss = jnp.sum(seg * seg, axis=-1, keepdims=True)
            seg = seg * lax.rsqrt(ss + RMS_EPS)
            if g < GDN_QK_HEADS:
                seg = seg * (GDN_HEAD_DIM ** -0.5)
        o_ref[0, :, g * LANES:(g + 1) * LANES] = seg


def _gdn_conv(y, conv_w):
    b, s, _ = y.shape
    c = GDN_CONV_DIM
    ts = 256
    pad = 8
    kern = functools.partial(_gdn_conv_kernel, ts=ts)
    return pl.pallas_call(
        kern,
        grid=(b, s // ts),
        in_specs=[
            pl.BlockSpec((1, ts, c), lambda bi, i: (bi, i, 0)),
            pl.BlockSpec((1, pad, c), lambda bi, i: (bi, jnp.maximum(i * (ts // pad) - 1, 0), 0)),
            pl.BlockSpec((GDN_CONV, c), lambda bi, i: (0, 0)),
        ],
        out_specs=pl.BlockSpec((1, ts, c), lambda bi, i: (bi, i, 0)),
        out_shape=jax.ShapeDtypeStruct((b, s, c), F32),
        scratch_shapes=[pltpu.VMEM((ts + pad, c), F32)],
        compiler_params=_params("parallel", "parallel"),
        name="gdn_conv",
    )(y, y, conv_w)


def _gdn_kernel(q_ref, k_ref, v_ref, z_ref, ext_ref, alog_ref, dtb_ref, gain_ref, o_ref, state_ref, *, rows):
    chunk = GDN_CHUNK
    hv = pl.program_id(1)

    @pl.when(pl.program_id(2) == 0)
    def _():
        state_ref[...] = jnp.zeros_like(state_ref)

    ext = ext_ref[0]
    g_all = -jnp.exp(alog_ref[...]) * _softplus(ext + dtb_ref[...])
    gc_all = _chunk_cumsum(g_all, chunk)
    lane = lax.broadcasted_iota(jnp.int32, (rows, LANES), 1)
    gc = jnp.sum(jnp.where(lane == hv, gc_all, 0.0), axis=-1, keepdims=True)
    beta = jnp.sum(jnp.where(lane == hv + GDN_V_HEADS, _sigmoid(ext), 0.0), axis=-1, keepdims=True)
    sub = lax.broadcasted_iota(jnp.int32, (LANES, rows), 0)
    gc_row = jnp.sum(jnp.where(sub == hv, gc_all.T, 0.0), axis=0, keepdims=True)

    q = q_ref[0]
    k = k_ref[0]
    v = v_ref[0]
    eg = jnp.exp(gc)
    kb = k * beta
    rhs = jnp.concatenate([v * beta, kb * eg], axis=-1)
    q_dec = (q * eg).astype(BF16)
    k16 = k.astype(BF16)
    q16 = q.astype(BF16)
    row = lax.broadcasted_iota(jnp.int32, (chunk, chunk), 0)
    col = lax.broadcasted_iota(jnp.int32, (chunk, chunk), 1)
    causal = row >= col
    strict = row > col
    eye = (row == col).astype(F32)
    d = GDN_HEAD_DIM
    for c in range(rows // chunk):
        sl = slice(c * chunk, (c + 1) * chunk)
        gcc = gc[sl]
        diff = gcc - gc_row[:, sl]
        decay = jnp.where(causal, jnp.exp(jnp.where(causal, diff, 0.0)), 0.0)
        kk = _dot_nt(k16[sl], k16[sl])
        qk = _dot_nt(q16[sl], k16[sl]) * decay
        nk = -jnp.where(strict, kk * beta[sl] * decay, 0.0)
        inv = eye + nk
        span = 2
        while span < chunk:
            nk = _dot(nk, nk, precision=HIGHEST)
            inv = inv + _dot(inv, nk, precision=HIGHEST)
            span *= 2
        uw = _dot(inv, rhs[sl], precision=HIGHEST)
        g_last = gcc[chunk - 1:chunk, :]
        k_dec = (k[sl] * jnp.exp(g_last - gcc)).astype(BF16)
        state = state_ref[...]
        s16 = state.astype(BF16)
        v_new = uw[:, :d] - _dot(uw[:, d:].astype(BF16), s16)
        v16 = v_new.astype(BF16)
        o = _dot(q_dec[sl], s16) + _dot(qk.astype(BF16), v16)
        state_ref[...] = state * jnp.exp(g_last) + _dot(k_dec.T, v16)
        ms = jnp.mean(o * o, axis=-1, keepdims=True)
        on = o * lax.rsqrt(ms + RMS_EPS) * gain_ref[...]
        o_ref[0, sl, :] = (on * _silu(z_ref[0, sl, :])).astype(o_ref.dtype)


def _gdn_layer(x2d, bsz, norm_w, w_in, conv_w, a_log, dt_bias, o_gain, w_out):
    t, dm = x2d.shape
    s = t // bsz
    hq, hv, d = GDN_QK_HEADS, GDN_V_HEADS, GDN_HEAD_DIM
    n_main = GDN_CONV_DIM + GDN_WIDTH
    w_main, w_ext = _split_w_in(w_in, n_main)
    y, ext = _inproj(x2d, norm_w, w_main, w_ext, jnp.ones((1, LANES), F32), 0, F32, "gdn_inproj")
    y = y.reshape(bsz, s, n_main)
    ext = ext.reshape(bsz, s, LANES)
    qkv = _gdn_conv(y, conv_w)
    alog = jnp.zeros((1, LANES), F32).at[0, :hv].set(a_log)
    dtb = jnp.zeros((1, LANES), F32).at[0, :hv].set(dt_bias)
    rows = 512
    rep = hv // hq
    kern = functools.partial(_gdn_kernel, rows=rows)
    z_off = GDN_CONV_DIM // d
    g = pl.pallas_call(
        kern,
        grid=(bsz, hv, s // rows),
        in_specs=[
            pl.BlockSpec((1, rows, d), lambda bi, hi, i: (bi, i, hi // rep)),
            pl.BlockSpec((1, rows, d), lambda bi, hi, i: (bi, i, hq + hi // rep)),
            pl.BlockSpec((1, rows, d), lambda bi, hi, i: (bi, i, 2 * hq + hi)),
            pl.BlockSpec((1, rows, d), lambda bi, hi, i: (bi, i, z_off + hi)),
            pl.BlockSpec((1, rows, LANES), lambda bi, hi, i: (bi, i, 0)),
            pl.BlockSpec((1, LANES), lambda bi, hi, i: (0, 0)),
            pl.BlockSpec((1, LANES), lambda bi, hi, i: (0, 0)),
            pl.BlockSpec((1, d), lambda bi, hi, i: (0, 0)),
        ],
        out_specs=pl.BlockSpec((1, rows, d), lambda bi, hi, i: (bi, i, hi)),
        out_shape=jax.ShapeDtypeStruct((bsz, s, GDN_WIDTH), BF16),
        scratch_shapes=[pltpu.VMEM((d, d), F32)],
        compiler_params=_params("parallel", "parallel", "arbitrary"),
        name="gdn_chunk",
    )(qkv, qkv, qkv, y, ext, alog, dtb, o_gain.reshape(1, -1))
    return _outproj(g.reshape(t, GDN_WIDTH), w_out, x2d, "gdn_outproj")


def kernel(x, norm_w, fox_w_in, fox_b_f, fox_q_gain, fox_k_gain, fox_w_out, gla_w_in, gla_w_gate_up, gla_b_gate, gla_o_gain, gla_w_out, gdn_w_in, gdn_conv_w, gdn_a_log, gdn_dt_bias, gdn_o_gain, gdn_w_out):
    bsz, s, d = x.shape
    x2d = x.reshape(bsz * s, d)
    for layer in range(DEPTH):
        kind, idx = layer % N_MIXERS, layer // N_MIXERS
        if kind == 0:
            x2d = _fox_layer(x2d, bsz, norm_w[layer], fox_w_in[idx], fox_b_f[idx], fox_q_gain[idx],
                             fox_k_gain[idx], fox_w_out[idx], str(idx))
        elif kind == 1:
            x2d = _gla_layer(x2d, bsz, norm_w[layer], gla_w_in[idx], gla_w_gate_up[idx], gla_b_gate[idx],
                             gla_o_gain[idx], gla_w_out[idx])
        else:
            x2d = _gdn_layer(x2d, bsz, norm_w[layer], gdn_w_in[idx], gdn_conv_w[idx], gdn_a_log[idx],
                             gdn_dt_bias[idx], gdn_o_gain[idx], gdn_w_out[idx])
    return x2d.reshape(bsz, s, d)
```

```python
import functools
import math

import jax
import jax.numpy as jnp
from jax import lax
from jax.experimental import pallas as pl
from jax.experimental.pallas import tpu as pltpu

F32 = jnp.float32
BF16 = jnp.bfloat16

D_MODEL = 1024
DEPTH = 4
N_MIXERS = 3
RMS_EPS = 1e-6
LANES = 128

FOX_HEADS = 8
FOX_HEAD_DIM = 128
FOX_WIDTH = 1024

GLA_HEADS = 4
GLA_KEY_DIM = 128
GLA_VAL_DIM = 256
GLA_WIDTH = 1024
GLA_RANK = 16
GLA_TAU = 16.0
GLA_CHUNK = 64

GDN_QK_HEADS = 4
GDN_V_HEADS = 8
GDN_HEAD_DIM = 128
GDN_WIDTH = 1024
GDN_CONV = 4
GDN_CHUNK = 64
GDN_CONV_DIM = 2048

VMEM_LIMIT_BYTES = 56 * 1024 * 1024

HIGHEST = lax.Precision.HIGHEST
LOG2_E = math.log2(math.e)

FOX_TQ, FOX_BQ, FOX_TK = 1024, 512, 512
GLA_ROWS = 512
GDN_ROWS = 256


def _params(*semantics):
    return pltpu.CompilerParams(dimension_semantics=semantics, vmem_limit_bytes=VMEM_LIMIT_BYTES)


def _dot(a, b, precision=None):
    return jnp.dot(a, b, preferred_element_type=F32, precision=precision)


def _dot_nt(a, b):
    return lax.dot_general(a, b, (((1,), (1,)), ((), ())), preferred_element_type=F32)


def _sigmoid(x):
    return 1.0 / (1.0 + jnp.exp(-x))


def _silu(x):
    return x * _sigmoid(x)


def _softplus(x):
    return jnp.maximum(x, 0.0) + jnp.log(1.0 + jnp.exp(-jnp.abs(x)))


def _log_sigmoid(x):
    return -_softplus(-x)


def _chunk_cumsum(x, chunk):
    row = lax.broadcasted_iota(jnp.int32, x.shape, 0) & (chunk - 1)
    shift = 1
    while shift < chunk:
        x = x + jnp.where(row >= shift, pltpu.roll(x, shift, axis=0), 0.0)
        shift *= 2
    return x


def _inproj_kernel(x_ref, nw_ref, w_ref, we_ref, gain_ref, o_ref, oe_ref, *, tn, n_norm_groups):
    x = x_ref[...]
    ms = jnp.mean(x * x, axis=-1, keepdims=True)
    xn = (x * lax.rsqrt(ms + RMS_EPS) * nw_ref[...]).astype(BF16)
    oe_ref[...] = _dot(xn, we_ref[...])
    n = o_ref.shape[1]
    for j in range(n // tn):
        acc = _dot(xn, w_ref[:, j * tn:(j + 1) * tn])
        for g in range(tn // LANES):
            col = j * tn + g * LANES
            seg = acc[:, g * LANES:(g + 1) * LANES]
            if col // LANES < n_norm_groups:
                msq = jnp.mean(seg * seg, axis=-1, keepdims=True)
                seg = seg * lax.rsqrt(msq + RMS_EPS) * gain_ref[:, col:col + LANES]
            o_ref[:, col:col + LANES] = seg.astype(o_ref.dtype)


def _inproj(x2d, norm_w, w_main, w_ext, gain, n_norm_groups, out_dtype, name):
    t, d = x2d.shape
    n = w_main.shape[1]
    tm, tn = 512, 512
    kern = functools.partial(_inproj_kernel, tn=tn, n_norm_groups=n_norm_groups)
    return pl.pallas_call(
        kern,
        grid=(t // tm,),
        in_specs=[
            pl.BlockSpec((tm, d), lambda i: (i, 0)),
            pl.BlockSpec((1, d), lambda i: (0, 0)),
            pl.BlockSpec((d, n), lambda i: (0, 0)),
            pl.BlockSpec((d, LANES), lambda i: (0, 0)),
            pl.BlockSpec((1, gain.shape[1]), lambda i: (0, 0)),
        ],
        out_specs=[
            pl.BlockSpec((tm, n), lambda i: (i, 0)),
            pl.BlockSpec((tm, LANES), lambda i: (i, 0)),
        ],
        out_shape=[
            jax.ShapeDtypeStruct((t, n), out_dtype),
            jax.ShapeDtypeStruct((t, LANES), F32),
        ],
        compiler_params=_params("parallel"),
        name=name,
    )(x2d, norm_w.reshape(1, d), w_main, w_ext, gain)


def _split_w_in(w_in, n_main):
    d, n = w_in.shape
    w_main = w_in[:, :n_main].astype(BF16)
    w_ext = jnp.zeros((d, LANES), F32).at[:, :n - n_main].set(w_in[:, n_main:]).astype(BF16)
    return w_main, w_ext


def _outproj_kernel(g_ref, w_ref, x_ref, o_ref):
    o_ref[...] = x_ref[...] + _dot(g_ref[...], w_ref[...])


def _outproj(g2d, w_out, x2d, name):
    t, d = x2d.shape
    k = g2d.shape[1]
    tm = 512
    return pl.pallas_call(
        _outproj_kernel,
        grid=(t // tm,),
        in_specs=[
            pl.BlockSpec((tm, k), lambda i: (i, 0)),
            pl.BlockSpec((k, d), lambda i: (0, 0)),
            pl.BlockSpec((tm, d), lambda i: (i, 0)),
        ],
        out_specs=pl.BlockSpec((tm, d), lambda i: (i, 0)),
        out_shape=jax.ShapeDtypeStruct((t, d), F32),
        compiler_params=_params("parallel"),
        name=name,
    )(g2d, w_out.astype(BF16), x2d)


def _fox_cumsum_kernel(f_ref, b_ref, tri_ref, ccol_ref, crow_ref, carry_ref):
    @pl.when(pl.program_id(1) == 0)
    def _():
        carry_ref[...] = jnp.zeros_like(carry_ref)

    log_f = _log_sigmoid(f_ref[0] + b_ref[...]) * LOG2_E
    cs = _dot(tri_ref[...], log_f, precision=HIGHEST) + carry_ref[...]
    ts = cs.shape[0]
    carry_ref[...] = cs[ts - 1:ts, :]
    ccol_ref[0] = cs
    crow_ref[0] = cs.T[:FOX_HEADS, :]


def _fox_cumsum(ext, b_f):
    b, s, _ = ext.shape
    ts = 256
    bias = jnp.zeros((1, LANES), F32).at[0, :FOX_HEADS].set(b_f)
    tri = jnp.tril(jnp.ones((ts, ts), F32))
    return pl.pallas_call(
        _fox_cumsum_kernel,
        grid=(b, s // ts),
        in_specs=[
            pl.BlockSpec((1, ts, LANES), lambda bi, i: (bi, i, 0)),
            pl.BlockSpec((1, LANES), lambda bi, i: (0, 0)),
            pl.BlockSpec((ts, ts), lambda bi, i: (0, 0)),
        ],
        out_specs=[
            pl.BlockSpec((1, ts, LANES), lambda bi, i: (bi, i, 0)),
            pl.BlockSpec((1, FOX_HEADS, ts), lambda bi, i: (bi, 0, i)),
        ],
        out_shape=[
            jax.ShapeDtypeStruct((b, s, LANES), F32),
            jax.ShapeDtypeStruct((b, FOX_HEADS, s), F32),
        ],
        scratch_shapes=[pltpu.VMEM((1, LANES), F32)],
        compiler_params=_params("parallel", "arbitrary"),
        name="fox_cumsum",
    )(ext, bias, tri)


def _fox_attn_kernel(q_ref, k_ref, v_ref, z_ref, ccol_ref, crow_ref, o_ref, *, tq, bq, tk):
    h = pl.program_id(1)
    i = pl.program_id(2)
    nsub = tq // bq
    lane = lax.broadcasted_iota(jnp.int32, (bq, LANES), 1)
    qs = [q_ref[0, r * bq:(r + 1) * bq, :] for r in range(nsub)]
    cqs = [jnp.sum(jnp.where(lane == h, ccol_ref[0, r * bq:(r + 1) * bq, :], 0.0), axis=-1, keepdims=True)
           for r in range(nsub)]
    row = lax.broadcasted_iota(jnp.int32, (bq, tk), 0)
    col = lax.broadcasted_iota(jnp.int32, (bq, tk), 1)

    def load_keys(start):
        return (k_ref[0, pl.ds(start, tk), :], v_ref[0, pl.ds(start, tk), :], crow_ref[0, :, pl.ds(start, tk)])

    def attend(r, carry, keys, row_minus_col):
        m, l, acc = carry
        k, v, ck = keys
        t = _dot_nt(qs[r], k) - ck
        if row_minus_col is not None:
            t = jnp.where(row + row_minus_col >= col, t, -jnp.inf)
        m_new = jnp.maximum(m, jnp.max(t, axis=-1, keepdims=True) + cqs[r])
        alpha = jnp.exp2(m - m_new)
        p = jnp.exp2(t + (cqs[r] - m_new))
        l = alpha * l + jnp.sum(p, axis=-1, keepdims=True)
        acc = alpha * acc + _dot(p.astype(BF16), v)
        return m_new, l, acc

    def below_diagonal(kb, carries):
        keys = load_keys(pl.multiple_of(kb * tk, tk))
        return tuple(attend(r, carries[r], keys, None) for r in range(nsub))

    init = tuple((jnp.full((bq, 1), -jnp.inf, F32), jnp.zeros((bq, 1), F32), jnp.zeros((bq, LANES), F32))
                 for _ in range(nsub))
    carries = list(lax.fori_loop(0, i * (tq // tk), below_diagonal, init))
    for c in range(tq // tk):
        keys = load_keys(pl.multiple_of(i * tq + c * tk, tk))
        for r in range(nsub):
            row0, col0 = r * bq, c * tk
            if col0 > row0 + bq - 1:
                continue
            crosses = col0 + tk - 1 > row0
            carries[r] = attend(r, carries[r], keys, (row0 - col0) if crosses else None)
    for r in range(nsub):
        _, l, acc = carries[r]
        z = z_ref[0, r * bq:(r + 1) * bq, :].astype(F32)
        o_ref[0, r * bq:(r + 1) * bq, :] = (acc / l * _silu(z)).astype(o_ref.dtype)


def _fox_attn(y, ccol, crow):
    b, s, _ = y.shape
    h = FOX_HEADS
    tq, bq, tk = min(FOX_TQ, s), FOX_BQ, FOX_TK
    crow3 = crow.reshape(b * h, 1, s)
    kern = functools.partial(_fox_attn_kernel, tq=tq, bq=bq, tk=tk)
    return pl.pallas_call(
        kern,
        grid=(b, h, s // tq),
        in_specs=[
            pl.BlockSpec((1, tq, LANES), lambda bi, hi, i: (bi, i, hi)),
            pl.BlockSpec((1, s, LANES), lambda bi, hi, i: (bi, 0, h + hi)),
            pl.BlockSpec((1, s, LANES), lambda bi, hi, i: (bi, 0, 2 * h + hi)),
            pl.BlockSpec((1, tq, LANES), lambda bi, hi, i: (bi, i, 3 * h + hi)),
            pl.BlockSpec((1, tq, LANES), lambda bi, hi, i: (bi, i, 0)),
            pl.BlockSpec((1, 1, s), lambda bi, hi, i: (bi * h + hi, 0, 0)),
        ],
        out_specs=pl.BlockSpec((1, tq, LANES), lambda bi, hi, i: (bi, i, hi)),
        out_shape=jax.ShapeDtypeStruct((b, s, FOX_WIDTH), BF16),
        compiler_params=_params("parallel", "parallel", "arbitrary"),
        name="fox_attn",
    )(y, y, y, y, ccol, crow3)


def _fox_layer(x2d, bsz, norm_w, w_in, b_f, q_gain, k_gain, w_out, tag):
    t, d = x2d.shape
    s = t // bsz
    w_main, w_ext = _split_w_in(w_in, 4 * FOX_WIDTH)
    scale = FOX_HEAD_DIM ** -0.5 * LOG2_E
    gain = jnp.concatenate([jnp.tile(q_gain * scale, FOX_HEADS), jnp.tile(k_gain, FOX_HEADS)]).reshape(1, -1)
    y, ext = _inproj(x2d, norm_w, w_main, w_ext, gain, 2 * FOX_HEADS, BF16, "fox_inproj" + tag)
    ccol, crow = _fox_cumsum(ext.reshape(bsz, s, LANES), b_f)
    g = _fox_attn(y.reshape(bsz, s, -1), ccol, crow)
    return _outproj(g.reshape(t, FOX_WIDTH), w_out, x2d, "fox_outproj" + tag)


def _gla_kernel(q_ref, k_ref, v_ref, z_ref, ext_ref, wup_ref, bg_ref, gain_ref, o_ref, state_ref, *, rows):
    chunk = GLA_CHUNK
    chunks = [slice(c * chunk, (c + 1) * chunk) for c in range(rows // chunk)]

    @pl.when(pl.program_id(2) == 0)
    def _():
        state_ref[...] = jnp.zeros_like(state_ref)

    gate = _dot(ext_ref[0].astype(BF16), wup_ref[...]) + bg_ref[...]
    log_a = _log_sigmoid(gate) * (1.0 / GLA_TAU)
    bcum = _chunk_cumsum(log_a, chunk)
    q = q_ref[0] * (GLA_KEY_DIM ** -0.5)
    k = k_ref[0]
    q_dec = (q * jnp.exp(bcum)).astype(BF16)
    k_inv = (k * jnp.exp(-bcum)).astype(BF16)
    row = lax.broadcasted_iota(jnp.int32, (chunk, chunk), 0)
    col = lax.broadcasted_iota(jnp.int32, (chunk, chunk), 1)
    causal = row >= col
    b_last = [bcum[sl][chunk - 1:chunk, :] for sl in chunks]
    k_dec = [(k[sl] * jnp.exp(b_last[c] - bcum[sl])).astype(BF16) for c, sl in enumerate(chunks)]
    v16 = [v_ref[0, sl, :].astype(BF16) for sl in chunks]
    attn = [jnp.where(causal, _dot_nt(q_dec[sl], k_inv[sl]), 0.0).astype(BF16) for sl in chunks]
    update = [_dot(v_ref[0, sl, :].T.astype(BF16), k_dec[c]) for c, sl in enumerate(chunks)]
    state_t = state_ref[...]
    s16 = []
    for c in range(len(chunks)):
        s16.append(state_t.astype(BF16))
        state_t = state_t * jnp.exp(b_last[c]) + update[c]
    state_ref[...] = state_t
    for c, sl in enumerate(chunks):
        o = _dot(attn[c], v16[c]) + _dot_nt(q_dec[sl], s16[c])
        ms = jnp.mean(o * o, axis=-1, keepdims=True)
        on = o * lax.rsqrt(ms + RMS_EPS) * gain_ref[...]
        o_ref[0, sl, :] = (on * _silu(z_ref[0, sl, :])).astype(o_ref.dtype)


def _gla_layer(x2d, bsz, norm_w, w_in, w_gate_up, b_gate, o_gain, w_out):
    t, d = x2d.shape
    s = t // bsz
    h, dk, dv = GLA_HEADS, GLA_KEY_DIM, GLA_VAL_DIM
    n_main = 2 * h * dk + 2 * GLA_WIDTH
    w_main, w_ext = _split_w_in(w_in, n_main)
    y, ext = _inproj(x2d, norm_w, w_main, w_ext, jnp.ones((1, LANES), F32), 0, F32, "gla_inproj")
    y = y.reshape(bsz, s, n_main)
    ext = ext.reshape(bsz, s, LANES)
    wup = jnp.zeros((LANES, h * dk), F32).at[:GLA_RANK].set(w_gate_up).astype(BF16)
    rows = min(GLA_ROWS, s)
    kern = functools.partial(_gla_kernel, rows=rows)
    k_off = h * dk // dk
    v_off = 2 * h * dk // dv
    z_off = (2 * h * dk + GLA_WIDTH) // dv
    g = pl.pallas_call(
        kern,
        grid=(bsz, h, s // rows),
        in_specs=[
            pl.BlockSpec((1, rows, dk), lambda bi, hi, i: (bi, i, hi)),
            pl.BlockSpec((1, rows, dk), lambda bi, hi, i: (bi, i, k_off + hi)),
            pl.BlockSpec((1, rows, dv), lambda bi, hi, i: (bi, i, v_off + hi)),
            pl.BlockSpec((1, rows, dv), lambda bi, hi, i: (bi, i, z_off + hi)),
            pl.BlockSpec((1, rows, LANES), lambda bi, hi, i: (bi, i, 0)),
            pl.BlockSpec((LANES, dk), lambda bi, hi, i: (0, hi)),
            pl.BlockSpec((1, dk), lambda bi, hi, i: (0, hi)),
            pl.BlockSpec((1, dv), lambda bi, hi, i: (0, 0)),
        ],
        out_specs=pl.BlockSpec((1, rows, dv), lambda bi, hi, i: (bi, i, hi)),
        out_shape=jax.ShapeDtypeStruct((bsz, s, GLA_WIDTH), BF16),
        scratch_shapes=[pltpu.VMEM((dv, dk), F32)],
        compiler_params=_params("parallel", "parallel", "arbitrary"),
        name="gla_chunk",
    )(y, y, y, y, ext, wup, b_gate.reshape(1, -1), o_gain.reshape(1, -1))
    return _outproj(g.reshape(t, GLA_WIDTH), w_out, x2d, "gla_outproj")


def _gdn_conv_kernel(u_ref, tail_ref, w_ref, o_ref, ext_ref, *, ts):
    pad = 8
    first = pl.program_id(1) == 0
    ext_ref[0:pad, :] = jnp.where(first, 0.0, tail_ref[0])
    ext_ref[pad:pad + ts, :] = u_ref[0]
    acc = jnp.zeros(u_ref.shape[1:], F32)
    for tap in range(GDN_CONV):
        off = pad - (GDN_CONV - 1) + tap
        acc = acc + ext_ref[off:off + ts, :] * w_ref[tap:tap + 1, :]
    y = _silu(acc)
    n_qk = 2 * GDN_QK_HEADS
    for g in range(GDN_CONV_DIM // LANES):
        seg = y[:, g * LANES:(g + 1) * LANES]
        if g < n_qk:
            ss = jnp.sum(seg * seg, axis=-1, keepdims=True)
            seg = seg * lax.rsqrt(ss + RMS_EPS)
            if g < GDN_QK_HEADS:
                seg = seg * (GDN_HEAD_DIM ** -0.5)
        o_ref[0, :, g * LANES:(g + 1) * LANES] = seg


def _gdn_conv(y, conv_w):
    b, s, _ = y.shape
    c = GDN_CONV_DIM
    ts = 256
    pad = 8
    kern = functools.partial(_gdn_conv_kernel, ts=ts)
    return pl.pallas_call(
        kern,
        grid=(b, s // ts),
        in_specs=[
            pl.BlockSpec((1, ts, c), lambda bi, i: (bi, i, 0)),
            pl.BlockSpec((1, pad, c), lambda bi, i: (bi, jnp.maximum(i * (ts // pad) - 1, 0), 0)),
            pl.BlockSpec((GDN_CONV, c), lambda bi, i: (0, 0)),
        ],
        out_specs=pl.BlockSpec((1, ts, c), lambda bi, i: (bi, i, 0)),
        out_shape=jax.ShapeDtypeStruct((b, s, c), F32),
        scratch_shapes=[pltpu.VMEM((ts + pad, c), F32)],
        compiler_params=_params("parallel", "parallel"),
        name="gdn_conv",
    )(y, y, conv_w)


def _gdn_kernel(qkv_ref, z_ref, ext_ref, alog_ref, dtb_ref, gain_ref, o_ref, state_ref, *, rows):
    chunk = GDN_CHUNK
    d = GDN_HEAD_DIM
    n_hq, n_hv = GDN_QK_HEADS, GDN_V_HEADS
    rep = n_hv // n_hq
    chunks = [slice(c * chunk, (c + 1) * chunk) for c in range(rows // chunk)]
    pairs = [(hv, c) for hv in range(n_hv) for c in range(len(chunks))]

    @pl.when(pl.program_id(1) == 0)
    def _():
        state_ref[...] = jnp.zeros_like(state_ref)

    ext = ext_ref[0]
    g_all = -jnp.exp(alog_ref[...]) * _softplus(ext + dtb_ref[...])
    gc_all = _chunk_cumsum(g_all, chunk)
    gc_all_t = gc_all.T
    beta_all = _sigmoid(ext)
    row = lax.broadcasted_iota(jnp.int32, (chunk, chunk), 0)
    col = lax.broadcasted_iota(jnp.int32, (chunk, chunk), 1)
    causal = row >= col
    strict = row > col

    q = [qkv_ref[0, :, h * d:(h + 1) * d] for h in range(n_hq)]
    k = [qkv_ref[0, :, (n_hq + h) * d:(n_hq + h + 1) * d] for h in range(n_hq)]
    q16 = [x.astype(BF16) for x in q]
    k16 = [x.astype(BF16) for x in k]
    kk = {(h, c): _dot_nt(k16[h][sl], k16[h][sl]) for h in range(n_hq) for c, sl in enumerate(chunks)}
    qk = {(h, c): _dot_nt(q16[h][sl], k16[h][sl]) for h in range(n_hq) for c, sl in enumerate(chunks)}

    gc, rhs, q_dec = [], [], []
    for hv in range(n_hv):
        h = hv // rep
        gc_h = gc_all[:, hv:hv + 1]
        beta = beta_all[:, n_hv + hv:n_hv + hv + 1]
        eg = jnp.exp(gc_h)
        v = qkv_ref[0, :, (2 * n_hq + hv) * d:(2 * n_hq + hv + 1) * d]
        gc.append(gc_h)
        rhs.append(jnp.concatenate([v * beta, k[h] * (beta * eg)], axis=-1))
        q_dec.append((q[h] * eg).astype(BF16))

    decay, nk, m = {}, {}, {}
    for hv, c in pairs:
        sl = chunks[c]
        diff = gc[hv][sl] - gc_all_t[hv:hv + 1, sl]
        decay[hv, c] = jnp.where(causal, jnp.exp(jnp.where(causal, diff, 0.0)), 0.0)
        beta = beta_all[sl, n_hv + hv:n_hv + hv + 1]
        nk[hv, c] = -jnp.where(strict, kk[hv // rep, c] * beta * decay[hv, c], 0.0)
        m[hv, c] = nk[hv, c]
    span = 2
    while span < chunk:
        n16 = {p: nk[p].astype(BF16) for p in pairs}
        nk = {p: _dot(n16[p], n16[p]) for p in pairs}
        mn = {p: _dot(m[p].astype(BF16), nk[p].astype(BF16)) for p in pairs}
        m = {p: m[p] + nk[p] + mn[p] for p in pairs}
        span *= 2
    uw = {(hv, c): rhs[hv][chunks[c]] + _dot(m[hv, c].astype(BF16), rhs[hv][chunks[c]].astype(BF16))
          for hv, c in pairs}

    heads = range(n_hv)
    state = [state_ref[hv] for hv in heads]
    for c, sl in enumerate(chunks):
        g_last = [gc[hv][sl][chunk - 1:chunk, :] for hv in heads]
        k_dec_t = [(k[hv // rep][sl] * jnp.exp(g_last[hv] - gc[hv][sl])).T.astype(BF16) for hv in heads]
        qk16 = [(qk[hv // rep, c] * decay[hv, c]).astype(BF16) for hv in heads]
        s16 = [state[hv].astype(BF16) for hv in heads]
        ws = [_dot(uw[hv, c][:, d:].astype(BF16), s16[hv]) for hv in heads]
        qs = [_dot(q_dec[hv][sl], s16[hv]) for hv in heads]
        v16 = [(uw[hv, c][:, :d] - ws[hv]).astype(BF16) for hv in heads]
        o = [qs[hv] + _dot(qk16[hv], v16[hv]) for hv in heads]
        state = [state[hv] * jnp.exp(g_last[hv]) + _dot(k_dec_t[hv], v16[hv]) for hv in heads]
        for hv in heads:
            ms = jnp.mean(o[hv] * o[hv], axis=-1, keepdims=True)
            on = o[hv] * lax.rsqrt(ms + RMS_EPS) * gain_ref[...]
            z = z_ref[0, sl, hv * d:(hv + 1) * d]
            o_ref[0, sl, hv * d:(hv + 1) * d] = (on * _silu(z)).astype(o_ref.dtype)
    for hv in heads:
        state_ref[hv] = state[hv]


def _gdn_layer(x2d, bsz, norm_w, w_in, conv_w, a_log, dt_bias, o_gain, w_out):
    t, dm = x2d.shape
    s = t // bsz
    hv, d = GDN_V_HEADS, GDN_HEAD_DIM
    n_main = GDN_CONV_DIM + GDN_WIDTH
    w_main, w_ext = _split_w_in(w_in, n_main)
    y, ext = _inproj(x2d, norm_w, w_main, w_ext, jnp.ones((1, LANES), F32), 0, F32, "gdn_inproj")
    y = y.reshape(bsz, s, n_main)
    ext = ext.reshape(bsz, s, LANES)
    qkv = _gdn_conv(y, conv_w)
    alog = jnp.zeros((1, LANES), F32).at[0, :hv].set(a_log)
    dtb = jnp.zeros((1, LANES), F32).at[0, :hv].set(dt_bias)
    rows = min(GDN_ROWS, s)
    kern = functools.partial(_gdn_kernel, rows=rows)
    g = pl.pallas_call(
        kern,
        grid=(bsz, s // rows),
        in_specs=[
            pl.BlockSpec((1, rows, GDN_CONV_DIM), lambda bi, i: (bi, i, 0)),
            pl.BlockSpec((1, rows, GDN_WIDTH), lambda bi, i: (bi, i, GDN_CONV_DIM // GDN_WIDTH)),
            pl.BlockSpec((1, rows, LANES), lambda bi, i: (bi, i, 0)),
            pl.BlockSpec((1, LANES), lambda bi, i: (0, 0)),
            pl.BlockSpec((1, LANES), lambda bi, i: (0, 0)),
            pl.BlockSpec((1, d), lambda bi, i: (0, 0)),
        ],
        out_specs=pl.BlockSpec((1, rows, GDN_WIDTH), lambda bi, i: (bi, i, 0)),
        out_shape=jax.ShapeDtypeStruct((bsz, s, GDN_WIDTH), BF16),
        scratch_shapes=[pltpu.VMEM((hv, d, d), F32)],
        compiler_params=_params("parallel", "arbitrary"),
        name="gdn_chunk",
    )(qkv, y, ext, alog, dtb, o_gain.reshape(1, -1))
    return _outproj(g.reshape(t, GDN_WIDTH), w_out, x2d, "gdn_outproj")


def kernel(x, norm_w, fox_w_in, fox_b_f, fox_q_gain, fox_k_gain, fox_w_out, gla_w_in, gla_w_gate_up, gla_b_gate, gla_o_gain, gla_w_out, gdn_w_in, gdn_conv_w, gdn_a_log, gdn_dt_bias, gdn_o_gain, gdn_w_out):
    bsz, s, d = x.shape
    x2d = x.reshape(bsz * s, d)
    for layer in range(DEPTH):
        kind, idx = layer % N_MIXERS, layer // N_MIXERS
        if kind == 0:
            x2d = _fox_layer(x2d, bsz, norm_w[layer], fox_w_in[idx], fox_b_f[idx], fox_q_gain[idx],
                             fox_k_gain[idx], fox_w_out[idx], str(idx))
        elif kind == 1:
            x2d = _gla_layer(x2d, bsz, norm_w[layer], gla_w_in[idx], gla_w_gate_up[idx], gla_b_gate[idx],
                             gla_o_gain[idx], gla_w_out[idx])
        else:
            x2d = _gdn_layer(x2d, bsz, norm_w[layer], gdn_w_in[idx], gdn_conv_w[idx], gdn_a_log[idx],
                             gdn_dt_bias[idx], gdn_o_gain[idx], gdn_w_out[idx])
    return x2d.reshape(bsz, s, d)
```

```python
import functools
import math

import jax
import jax.numpy as jnp
from jax import lax
from jax.experimental import pallas as pl
from jax.experimental.pallas import tpu as pltpu

F32 = jnp.float32
BF16 = jnp.bfloat16

D_MODEL = 1024
DEPTH = 4
N_MIXERS = 3
RMS_EPS = 1e-6
LANES = 128

FOX_HEADS = 8
FOX_HEAD_DIM = 128
FOX_WIDTH = 1024

GLA_HEADS = 4
GLA_KEY_DIM = 128
GLA_VAL_DIM = 256
GLA_WIDTH = 1024
GLA_RANK = 16
GLA_TAU = 16.0
GLA_CHUNK = 64

GDN_QK_HEADS = 4
GDN_V_HEADS = 8
GDN_HEAD_DIM = 128
GDN_WIDTH = 1024
GDN_CONV = 4
GDN_CHUNK = 64
GDN_CONV_DIM = 2048

VMEM_LIMIT_BYTES = 56 * 1024 * 1024

HIGHEST = lax.Precision.HIGHEST
LOG2_E = math.log2(math.e)

FOX_TQ, FOX_BQ, FOX_TK = 1024, 512, 512
FOX_BOUNDED_TQ = 2048
FOX_MAX_FIXED_SHIFT = 40.0
GLA_ROWS = 512
GDN_ROWS = 256


def _params(*semantics):
    return pltpu.CompilerParams(dimension_semantics=semantics, vmem_limit_bytes=VMEM_LIMIT_BYTES)


def _dot(a, b, precision=None):
    return jnp.dot(a, b, preferred_element_type=F32, precision=precision)


def _dot_nt(a, b):
    return lax.dot_general(a, b, (((1,), (1,)), ((), ())), preferred_element_type=F32)


def _sigmoid(x):
    return 1.0 / (1.0 + jnp.exp(-x))


def _silu(x):
    return x * _sigmoid(x)


def _softplus(x):
    return jnp.maximum(x, 0.0) + jnp.log(1.0 + jnp.exp(-jnp.abs(x)))


def _log_sigmoid(x):
    return -_softplus(-x)


def _chunk_cumsum(x, chunk):
    row = lax.broadcasted_iota(jnp.int32, x.shape, 0) & (chunk - 1)
    shift = 1
    while shift < chunk:
        x = x + jnp.where(row >= shift, pltpu.roll(x, shift, axis=0), 0.0)
        shift *= 2
    return x


def _inproj_kernel(x_ref, nw_ref, w_ref, we_ref, gain_ref, o_ref, oe_ref, *, tn, n_norm_groups):
    x = x_ref[...]
    ms = jnp.mean(x * x, axis=-1, keepdims=True)
    xn = (x * lax.rsqrt(ms + RMS_EPS) * nw_ref[...]).astype(BF16)
    oe_ref[...] = _dot(xn, we_ref[...])
    n = o_ref.shape[1]
    for j in range(n // tn):
        acc = _dot(xn, w_ref[:, j * tn:(j + 1) * tn])
        for g in range(tn // LANES):
            col = j * tn + g * LANES
            seg = acc[:, g * LANES:(g + 1) * LANES]
            if col // LANES < n_norm_groups:
                msq = jnp.mean(seg * seg, axis=-1, keepdims=True)
                seg = seg * lax.rsqrt(msq + RMS_EPS) * gain_ref[:, col:col + LANES]
            o_ref[:, col:col + LANES] = seg.astype(o_ref.dtype)


def _inproj(x2d, norm_w, w_main, w_ext, gain, n_norm_groups, out_dtype, name):
    t, d = x2d.shape
    n = w_main.shape[1]
    tm, tn = 512, 512
    kern = functools.partial(_inproj_kernel, tn=tn, n_norm_groups=n_norm_groups)
    return pl.pallas_call(
        kern,
        grid=(t // tm,),
        in_specs=[
            pl.BlockSpec((tm, d), lambda i: (i, 0)),
            pl.BlockSpec((1, d), lambda i: (0, 0)),
            pl.BlockSpec((d, n), lambda i: (0, 0)),
            pl.BlockSpec((d, LANES), lambda i: (0, 0)),
            pl.BlockSpec((1, gain.shape[1]), lambda i: (0, 0)),
        ],
        out_specs=[
            pl.BlockSpec((tm, n), lambda i: (i, 0)),
            pl.BlockSpec((tm, LANES), lambda i: (i, 0)),
        ],
        out_shape=[
            jax.ShapeDtypeStruct((t, n), out_dtype),
            jax.ShapeDtypeStruct((t, LANES), F32),
        ],
        compiler_params=_params("parallel"),
        name=name,
    )(x2d, norm_w.reshape(1, d), w_main, w_ext, gain)


def _split_w_in(w_in, n_main):
    d, n = w_in.shape
    w_main = w_in[:, :n_main].astype(BF16)
    w_ext = jnp.zeros((d, LANES), F32).at[:, :n - n_main].set(w_in[:, n_main:]).astype(BF16)
    return w_main, w_ext


def _outproj_kernel(g_ref, w_ref, x_ref, o_ref):
    o_ref[...] = x_ref[...] + _dot(g_ref[...], w_ref[...])


def _outproj(g2d, w_out, x2d, name):
    t, d = x2d.shape
    k = g2d.shape[1]
    tm = 512
    return pl.pallas_call(
        _outproj_kernel,
        grid=(t // tm,),
        in_specs=[
            pl.BlockSpec((tm, k), lambda i: (i, 0)),
            pl.BlockSpec((k, d), lambda i: (0, 0)),
            pl.BlockSpec((tm, d), lambda i: (i, 0)),
        ],
        out_specs=pl.BlockSpec((tm, d), lambda i: (i, 0)),
        out_shape=jax.ShapeDtypeStruct((t, d), F32),
        compiler_params=_params("parallel"),
        name=name,
    )(g2d, w_out.astype(BF16), x2d)


def _fox_cumsum_kernel(f_ref, b_ref, tri_ref, ccol_ref, crow_ref, carry_ref):
    @pl.when(pl.program_id(1) == 0)
    def _():
        carry_ref[...] = jnp.zeros_like(carry_ref)

    log_f = _log_sigmoid(f_ref[0] + b_ref[...]) * LOG2_E
    cs = _dot(tri_ref[...], log_f, precision=HIGHEST) + carry_ref[...]
    ts = cs.shape[0]
    carry_ref[...] = cs[ts - 1:ts, :]
    ccol_ref[0] = cs
    crow_ref[0] = cs.T[:FOX_HEADS, :]


def _fox_cumsum(ext, b_f):
    b, s, _ = ext.shape
    ts = 256
    bias = jnp.zeros((1, LANES), F32).at[0, :FOX_HEADS].set(b_f)
    tri = jnp.tril(jnp.ones((ts, ts), F32))
    return pl.pallas_call(
        _fox_cumsum_kernel,
        grid=(b, s // ts),
        in_specs=[
            pl.BlockSpec((1, ts, LANES), lambda bi, i: (bi, i, 0)),
            pl.BlockSpec((1, LANES), lambda bi, i: (0, 0)),
            pl.BlockSpec((ts, ts), lambda bi, i: (0, 0)),
        ],
        out_specs=[
            pl.BlockSpec((1, ts, LANES), lambda bi, i: (bi, i, 0)),
            pl.BlockSpec((1, FOX_HEADS, ts), lambda bi, i: (bi, 0, i)),
        ],
        out_shape=[
            jax.ShapeDtypeStruct((b, s, LANES), F32),
            jax.ShapeDtypeStruct((b, FOX_HEADS, s), F32),
        ],
        scratch_shapes=[pltpu.VMEM((1, LANES), F32)],
        compiler_params=_params("parallel", "arbitrary"),
        name="fox_cumsum",
    )(ext, bias, tri)


def _fox_attn_kernel(q_ref, k_ref, v_ref, z_ref, ccol_ref, crow_ref, o_ref, *, tq, bq, tk):
    h = pl.program_id(1)
    i = pl.program_id(2)
    nsub = tq // bq
    lane = lax.broadcasted_iota(jnp.int32, (bq, LANES), 1)
    qs = [q_ref[0, r * bq:(r + 1) * bq, :] for r in range(nsub)]
    cqs = [jnp.sum(jnp.where(lane == h, ccol_ref[0, r * bq:(r + 1) * bq, :], 0.0), axis=-1, keepdims=True)
           for r in range(nsub)]
    row = lax.broadcasted_iota(jnp.int32, (bq, tk), 0)
    col = lax.broadcasted_iota(jnp.int32, (bq, tk), 1)

    def load_keys(start):
        return (k_ref[0, pl.ds(start, tk), :], v_ref[0, pl.ds(start, tk), :], crow_ref[0, :, pl.ds(start, tk)])

    def attend(r, carry, keys, row_minus_col):
        m, l, acc = carry
        k, v, ck = keys
        t = _dot_nt(qs[r], k) - ck
        if row_minus_col is not None:
            t = jnp.where(row + row_minus_col >= col, t, -jnp.inf)
        m_new = jnp.maximum(m, jnp.max(t, axis=-1, keepdims=True) + cqs[r])
        alpha = jnp.exp2(m - m_new)
        p = jnp.exp2(t + (cqs[r] - m_new))
        l = alpha * l + jnp.sum(p, axis=-1, keepdims=True)
        acc = alpha * acc + _dot(p.astype(BF16), v)
        return m_new, l, acc

    def below_diagonal(kb, carries):
        keys = load_keys(pl.multiple_of(kb * tk, tk))
        return tuple(attend(r, carries[r], keys, None) for r in range(nsub))

    init = tuple((jnp.full((bq, 1), -jnp.inf, F32), jnp.zeros((bq, 1), F32), jnp.zeros((bq, LANES), F32))
                 for _ in range(nsub))
    carries = list(lax.fori_loop(0, i * (tq // tk), below_diagonal, init))
    for c in range(tq // tk):
        keys = load_keys(pl.multiple_of(i * tq + c * tk, tk))
        for r in range(nsub):
            row0, col0 = r * bq, c * tk
            if col0 > row0 + bq - 1:
                continue
            crosses = col0 + tk - 1 > row0
            carries[r] = attend(r, carries[r], keys, (row0 - col0) if crosses else None)
    for r in range(nsub):
        _, l, acc = carries[r]
        z = z_ref[0, r * bq:(r + 1) * bq, :].astype(F32)
        o_ref[0, r * bq:(r + 1) * bq, :] = (acc / l * _silu(z)).astype(o_ref.dtype)


def _fox_attn_bounded_kernel(q_ref, k_ref, v_ref, z_ref, ccol_ref, crow_ref, bound_ref, o_ref, *, tq, tk, unroll):
    h = pl.program_id(1)
    i = pl.program_id(2)
    nsub = tq // tk
    lane = lax.broadcasted_iota(jnp.int32, (tq, LANES), 1)
    cq = jnp.sum(jnp.where(lane == h, ccol_ref[0], 0.0), axis=-1, keepdims=True)
    dq = cq - bound_ref[0:1, 0:1]
    q = q_ref[0]
    ones = jnp.ones((tk, LANES), BF16)

    def weighted_values(start, first_row, diagonal):
        k = k_ref[0, pl.ds(start, tk), :]
        v1 = jnp.concatenate([v_ref[0, pl.ds(start, tk), :], ones], axis=1)
        ck = crow_ref[0, :, pl.ds(start, tk)]
        e = (_dot_nt(q[first_row:], k) - ck) + dq[first_row:]
        if diagonal:
            row = lax.broadcasted_iota(jnp.int32, (tk, tk), 0)
            col = lax.broadcasted_iota(jnp.int32, (tk, tk), 1)
            top = jnp.where(row >= col, e[:tk], -jnp.inf)
            e = top if e.shape[0] == tk else jnp.concatenate([top, e[tk:]], axis=0)
        return _dot(jnp.exp2(e).astype(BF16), v1)

    def below_diagonal(jj, acc):
        for u in range(unroll):
            acc = acc + weighted_values(pl.multiple_of((jj * unroll + u) * tk, tk), 0, False)
        return acc

    acc = lax.fori_loop(0, i * nsub // unroll, below_diagonal, jnp.zeros((tq, 2 * LANES), F32))
    for c in range(nsub):
        pv = weighted_values(pl.multiple_of(i * tq + c * tk, tk), c * tk, True)
        acc = acc + pv if c == 0 else jnp.concatenate([acc[:c * tk], acc[c * tk:] + pv], axis=0)
    z = z_ref[0].astype(F32)
    o_ref[0] = (acc[:, :LANES] / acc[:, LANES:LANES + 1] * _silu(z)).astype(o_ref.dtype)


def _fox_attn(y, ccol, crow, logit_bound):
    b, s, _ = y.shape
    h = FOX_HEADS
    crow3 = crow.reshape(b * h, 1, s)

    def specs(tq):
        return [
            pl.BlockSpec((1, tq, LANES), lambda bi, hi, i: (bi, i, hi)),
            pl.BlockSpec((1, s, LANES), lambda bi, hi, i: (bi, 0, h + hi)),
            pl.BlockSpec((1, s, LANES), lambda bi, hi, i: (bi, 0, 2 * h + hi)),
            pl.BlockSpec((1, tq, LANES), lambda bi, hi, i: (bi, i, 3 * h + hi)),
            pl.BlockSpec((1, tq, LANES), lambda bi, hi, i: (bi, i, 0)),
            pl.BlockSpec((1, 1, s), lambda bi, hi, i: (bi * h + hi, 0, 0)),
        ]

    def online():
        tq, bq, tk = min(FOX_TQ, s), FOX_BQ, FOX_TK
        return pl.pallas_call(
            functools.partial(_fox_attn_kernel, tq=tq, bq=bq, tk=tk),
            grid=(b, h, s // tq),
            in_specs=specs(tq),
            out_specs=pl.BlockSpec((1, tq, LANES), lambda bi, hi, i: (bi, i, hi)),
            out_shape=jax.ShapeDtypeStruct((b, s, FOX_WIDTH), BF16),
            compiler_params=_params("parallel", "parallel", "arbitrary"),
            name="fox_attn",
        )(y, y, y, y, ccol, crow3)

    def bounded():
        tq, tk = min(FOX_BOUNDED_TQ, s), FOX_TK
        unroll = tq // tk
        return pl.pallas_call(
            functools.partial(_fox_attn_bounded_kernel, tq=tq, tk=tk, unroll=unroll),
            grid=(b, h, s // tq),
            in_specs=specs(tq) + [pl.BlockSpec((1, LANES), lambda bi, hi, i: (0, 0))],
            out_specs=pl.BlockSpec((1, tq, LANES), lambda bi, hi, i: (bi, i, hi)),
            out_shape=jax.ShapeDtypeStruct((b, s, FOX_WIDTH), BF16),
            compiler_params=_params("parallel", "parallel", "arbitrary"),
            name="fox_attn_bounded",
        )(y, y, y, y, ccol, crow3, jnp.full((1, LANES), logit_bound, F32))

    return lax.cond(logit_bound <= FOX_MAX_FIXED_SHIFT, bounded, online)


def _fox_layer(x2d, bsz, norm_w, w_in, b_f, q_gain, k_gain, w_out, tag):
    t, d = x2d.shape
    s = t // bsz
    w_main, w_ext = _split_w_in(w_in, 4 * FOX_WIDTH)
    scale = FOX_HEAD_DIM ** -0.5 * LOG2_E
    gain = jnp.concatenate([jnp.tile(q_gain * scale, FOX_HEADS), jnp.tile(k_gain, FOX_HEADS)]).reshape(1, -1)
    y, ext = _inproj(x2d, norm_w, w_main, w_ext, gain, 2 * FOX_HEADS, BF16, "fox_inproj" + tag)
    ccol, crow = _fox_cumsum(ext.reshape(bsz, s, LANES), b_f)
    logit_bound = FOX_HEAD_DIM * jnp.max(jnp.abs(q_gain * scale)) * jnp.max(jnp.abs(k_gain)) * 1.02
    g = _fox_attn(y.reshape(bsz, s, -1), ccol, crow, logit_bound)
    return _outproj(g.reshape(t, FOX_WIDTH), w_out, x2d, "fox_outproj" + tag)


def _gla_kernel(q_ref, k_ref, v_ref, z_ref, ext_ref, wup_ref, bg_ref, gain_ref, o_ref, state_ref, *, rows):
    chunk = GLA_CHUNK
    chunks = [slice(c * chunk, (c + 1) * chunk) for c in range(rows // chunk)]

    @pl.when(pl.program_id(2) == 0)
    def _():
        state_ref[...] = jnp.zeros_like(state_ref)

    gate = _dot(ext_ref[0].astype(BF16), wup_ref[...]) + bg_ref[...]
    log_a = _log_sigmoid(gate) * (1.0 / GLA_TAU)
    bcum = _chunk_cumsum(log_a, chunk)
    q = q_ref[0] * (GLA_KEY_DIM ** -0.5)
    k = k_ref[0]
    q_dec = (q * jnp.exp(bcum)).astype(BF16)
    k_inv = (k * jnp.exp(-bcum)).astype(BF16)
    row = lax.broadcasted_iota(jnp.int32, (chunk, chunk), 0)
    col = lax.broadcasted_iota(jnp.int32, (chunk, chunk), 1)
    causal = row >= col
    b_last = [bcum[sl][chunk - 1:chunk, :] for sl in chunks]
    k_dec = [(k[sl] * jnp.exp(b_last[c] - bcum[sl])).astype(BF16) for c, sl in enumerate(chunks)]
    v16 = [v_ref[0, sl, :].astype(BF16) for sl in chunks]
    attn = [jnp.where(causal, _dot_nt(q_dec[sl], k_inv[sl]), 0.0).astype(BF16) for sl in chunks]
    update = [_dot(v_ref[0, sl, :].T.astype(BF16), k_dec[c]) for c, sl in enumerate(chunks)]
    state_t = state_ref[...]
    s16 = []
    for c in range(len(chunks)):
        s16.append(state_t.astype(BF16))
        state_t = state_t * jnp.exp(b_last[c]) + update[c]
    state_ref[...] = state_t
    for c, sl in enumerate(chunks):
        o = _dot(attn[c], v16[c]) + _dot_nt(q_dec[sl], s16[c])
        ms = jnp.mean(o * o, axis=-1, keepdims=True)
        on = o * lax.rsqrt(ms + RMS_EPS) * gain_ref[...]
        o_ref[0, sl, :] = (on * _silu(z_ref[0, sl, :])).astype(o_ref.dtype)


def _gla_layer(x2d, bsz, norm_w, w_in, w_gate_up, b_gate, o_gain, w_out):
    t, d = x2d.shape
    s = t // bsz
    h, dk, dv = GLA_HEADS, GLA_KEY_DIM, GLA_VAL_DIM
    n_main = 2 * h * dk + 2 * GLA_WIDTH
    w_main, w_ext = _split_w_in(w_in, n_main)
    y, ext = _inproj(x2d, norm_w, w_main, w_ext, jnp.ones((1, LANES), F32), 0, F32, "gla_inproj")
    y = y.reshape(bsz, s, n_main)
    ext = ext.reshape(bsz, s, LANES)
    wup = jnp.zeros((LANES, h * dk), F32).at[:GLA_RANK].set(w_gate_up).astype(BF16)
    rows = min(GLA_ROWS, s)
    kern = functools.partial(_gla_kernel, rows=rows)
    k_off = h * dk // dk
    v_off = 2 * h * dk // dv
    z_off = (2 * h * dk + GLA_WIDTH) // dv
    g = pl.pallas_call(
        kern,
        grid=(bsz, h, s // rows),
        in_specs=[
            pl.BlockSpec((1, rows, dk), lambda bi, hi, i: (bi, i, hi)),
            pl.BlockSpec((1, rows, dk), lambda bi, hi, i: (bi, i, k_off + hi)),
            pl.BlockSpec((1, rows, dv), lambda bi, hi, i: (bi, i, v_off + hi)),
            pl.BlockSpec((1, rows, dv), lambda bi, hi, i: (bi, i, z_off + hi)),
            pl.BlockSpec((1, rows, LANES), lambda bi, hi, i: (bi, i, 0)),
            pl.BlockSpec((LANES, dk), lambda bi, hi, i: (0, hi)),
            pl.BlockSpec((1, dk), lambda bi, hi, i: (0, hi)),
            pl.BlockSpec((1, dv), lambda bi, hi, i: (0, 0)),
        ],
        out_specs=pl.BlockSpec((1, rows, dv), lambda bi, hi, i: (bi, i, hi)),
        out_shape=jax.ShapeDtypeStruct((bsz, s, GLA_WIDTH), BF16),
        scratch_shapes=[pltpu.VMEM((dv, dk), F32)],
        compiler_params=_params("parallel", "parallel", "arbitrary"),
        name="gla_chunk",
    )(y, y, y, y, ext, wup, b_gate.reshape(1, -1), o_gain.reshape(1, -1))
    return _outproj(g.reshape(t, GLA_WIDTH), w_out, x2d, "gla_outproj")


def _gdn_conv_kernel(u_ref, tail_ref, w_ref, o_ref, ext_ref, *, ts):
    pad = 8
    first = pl.program_id(1) == 0
    ext_ref[0:pad, :] = jnp.where(first, 0.0, tail_ref[0])
    ext_ref[pad:pad + ts, :] = u_ref[0]
    acc = jnp.zeros(u_ref.shape[1:], F32)
    for tap in range(GDN_CONV):
        off = pad - (GDN_CONV - 1) + tap
        acc = acc + ext_ref[off:off + ts, :] * w_ref[tap:tap + 1, :]
    y = _silu(acc)
    n_qk = 2 * GDN_QK_HEADS
    for g in range(GDN_CONV_DIM // LANES):
        seg = y[:, g * LANES:(g + 1) * LANES]
        if g < n_qk:
            ss = jnp.sum(seg * seg, axis=-1, keepdims=True)
            seg = seg * lax.rsqrt(ss + RMS_EPS)
            if g < GDN_QK_HEADS:
                seg = seg * (GDN_HEAD_DIM ** -0.5)
        o_ref[0, :, g * LANES:(g + 1) * LANES] = seg


def _gdn_conv(y, conv_w):
    b, s, _ = y.shape
    c = GDN_CONV_DIM
    ts = 256
    pad = 8
    kern = functools.partial(_gdn_conv_kernel, ts=ts)
    return pl.pallas_call(
        kern,
        grid=(b, s // ts),
        in_specs=[
            pl.BlockSpec((1, ts, c), lambda bi, i: (bi, i, 0)),
            pl.BlockSpec((1, pad, c), lambda bi, i: (bi, jnp.maximum(i * (ts // pad) - 1, 0), 0)),
            pl.BlockSpec((GDN_CONV, c), lambda bi, i: (0, 0)),
        ],
        out_specs=pl.BlockSpec((1, ts, c), lambda bi, i: (bi, i, 0)),
        out_shape=jax.ShapeDtypeStruct((b, s, c), F32),
        scratch_shapes=[pltpu.VMEM((ts + pad, c), F32)],
        compiler_params=_params("parallel", "parallel"),
        name="gdn_conv",
    )(y, y, conv_w)


def _gdn_kernel(qkv_ref, z_ref, ext_ref, alog_ref, dtb_ref, gain_ref, o_ref, state_ref, *, rows):
    chunk = GDN_CHUNK
    d = GDN_HEAD_DIM
    n_hq, n_hv = GDN_QK_HEADS, GDN_V_HEADS
    rep = n_hv // n_hq
    chunks = [slice(c * chunk, (c + 1) * chunk) for c in range(rows // chunk)]
    pairs = [(hv, c) for hv in range(n_hv) for c in range(len(chunks))]

    @pl.when(pl.program_id(1) == 0)
    def _():
        state_ref[...] = jnp.zeros_like(state_ref)

    ext = ext_ref[0]
    g_all = -jnp.exp(alog_ref[...]) * _softplus(ext + dtb_ref[...])
    gc_all = _chunk_cumsum(g_all, chunk)
    gc_all_t = gc_all.T
    beta_all = _sigmoid(ext)
    row = lax.broadcasted_iota(jnp.int32, (chunk, chunk), 0)
    col = lax.broadcasted_iota(jnp.int32, (chunk, chunk), 1)
    causal = row >= col
    strict = row > col

    q = [qkv_ref[0, :, h * d:(h + 1) * d] for h in range(n_hq)]
    k = [qkv_ref[0, :, (n_hq + h) * d:(n_hq + h + 1) * d] for h in range(n_hq)]
    q16 = [x.astype(BF16) for x in q]
    k16 = [x.astype(BF16) for x in k]
    kk = {(h, c): _dot_nt(k16[h][sl], k16[h][sl]) for h in range(n_hq) for c, sl in enumerate(chunks)}
    qk = {(h, c): _dot_nt(q16[h][sl], k16[h][sl]) for h in range(n_hq) for c, sl in enumerate(chunks)}

    gc, rhs, q_dec = [], [], []
    for hv in range(n_hv):
        h = hv // rep
        gc_h = gc_all[:, hv:hv + 1]
        beta = beta_all[:, n_hv + hv:n_hv + hv + 1]
        eg = jnp.exp(gc_h)
        v = qkv_ref[0, :, (2 * n_hq + hv) * d:(2 * n_hq + hv + 1) * d]
        gc.append(gc_h)
        rhs.append(jnp.concatenate([v * beta, k[h] * (beta * eg)], axis=-1))
        q_dec.append((q[h] * eg).astype(BF16))

    decay, nk, m = {}, {}, {}
    for hv, c in pairs:
        sl = chunks[c]
        diff = gc[hv][sl] - gc_all_t[hv:hv + 1, sl]
        decay[hv, c] = jnp.where(causal, jnp.exp(jnp.where(causal, diff, 0.0)), 0.0)
        beta = beta_all[sl, n_hv + hv:n_hv + hv + 1]
        nk[hv, c] = -jnp.where(strict, kk[hv // rep, c] * beta * decay[hv, c], 0.0)
        m[hv, c] = nk[hv, c]
    span = 2
    while span < chunk:
        n16 = {p: nk[p].astype(BF16) for p in pairs}
        nk = {p: _dot(n16[p], n16[p]) for p in pairs}
        mn = {p: _dot(m[p].astype(BF16), nk[p].astype(BF16)) for p in pairs}
        m = {p: m[p] + nk[p] + mn[p] for p in pairs}
        span *= 2
    uw = {(hv, c): rhs[hv][chunks[c]] + _dot(m[hv, c].astype(BF16), rhs[hv][chunks[c]].astype(BF16))
          for hv, c in pairs}

    heads = range(n_hv)
    state = [state_ref[hv] for hv in heads]
    for c, sl in enumerate(chunks):
        g_last = [gc[hv][sl][chunk - 1:chunk, :] for hv in heads]
        k_dec_t = [(k[hv // rep][sl] * jnp.exp(g_last[hv] - gc[hv][sl])).T.astype(BF16) for hv in heads]
        qk16 = [(qk[hv // rep, c] * decay[hv, c]).astype(BF16) for hv in heads]
        s16 = [state[hv].astype(BF16) for hv in heads]
        ws = [_dot(uw[hv, c][:, d:].astype(BF16), s16[hv]) for hv in heads]
        qs = [_dot(q_dec[hv][sl], s16[hv]) for hv in heads]
        v16 = [(uw[hv, c][:, :d] - ws[hv]).astype(BF16) for hv in heads]
        o = [qs[hv] + _dot(qk16[hv], v16[hv]) for hv in heads]
        state = [state[hv] * jnp.exp(g_last[hv]) + _dot(k_dec_t[hv], v16[hv]) for hv in heads]
        for hv in heads:
            ms = jnp.mean(o[hv] * o[hv], axis=-1, keepdims=True)
            on = o[hv] * lax.rsqrt(ms + RMS_EPS) * gain_ref[...]
            z = z_ref[0, sl, hv * d:(hv + 1) * d]
            o_ref[0, sl, hv * d:(hv + 1) * d] = (on * _silu(z)).astype(o_ref.dtype)
    for hv in heads:
        state_ref[hv] = state[hv]


def _gdn_layer(x2d, bsz, norm_w, w_in, conv_w, a_log, dt_bias, o_gain, w_out):
    t, dm = x2d.shape
    s = t // bsz
    hv, d = GDN_V_HEADS, GDN_HEAD_DIM
    n_main = GDN_CONV_DIM + GDN_WIDTH
    w_main, w_ext = _split_w_in(w_in, n_main)
    y, ext = _inproj(x2d, norm_w, w_main, w_ext, jnp.ones((1, LANES), F32), 0, F32, "gdn_inproj")
    y = y.reshape(bsz, s, n_main)
    ext = ext.reshape(bsz, s, LANES)
    qkv = _gdn_conv(y, conv_w)
    alog = jnp.zeros((1, LANES), F32).at[0, :hv].set(a_log)
    dtb = jnp.zeros((1, LANES), F32).at[0, :hv].set(dt_bias)
    rows = min(GDN_ROWS, s)
    kern = functools.partial(_gdn_kernel, rows=rows)
    g = pl.pallas_call(
        kern,
        grid=(bsz, s // rows),
        in_specs=[
            pl.BlockSpec((1, rows, GDN_CONV_DIM), lambda bi, i: (bi, i, 0)),
            pl.BlockSpec((1, rows, GDN_WIDTH), lambda bi, i: (bi, i, GDN_CONV_DIM // GDN_WIDTH)),
            pl.BlockSpec((1, rows, LANES), lambda bi, i: (bi, i, 0)),
            pl.BlockSpec((1, LANES), lambda bi, i: (0, 0)),
            pl.BlockSpec((1, LANES), lambda bi, i: (0, 0)),
            pl.BlockSpec((1, d), lambda bi, i: (0, 0)),
        ],
        out_specs=pl.BlockSpec((1, rows, GDN_WIDTH), lambda bi, i: (bi, i, 0)),
        out_shape=jax.ShapeDtypeStruct((bsz, s, GDN_WIDTH), BF16),
        scratch_shapes=[pltpu.VMEM((hv, d, d), F32)],
        compiler_params=_params("parallel", "arbitrary"),
        name="gdn_chunk",
    )(qkv, y, ext, alog, dtb, o_gain.reshape(1, -1))
    return _outproj(g.reshape(t, GDN_WIDTH), w_out, x2d, "gdn_outproj")


def kernel(x, norm_w, fox_w_in, fox_b_f, fox_q_gain, fox_k_gain, fox_w_out, gla_w_in, gla_w_gate_up, gla_b_gate, gla_o_gain, gla_w_out, gdn_w_in, gdn_conv_w, gdn_a_log, gdn_dt_bias, gdn_o_gain, gdn_w_out):
    bsz, s, d = x.shape
    x2d = x.reshape(bsz * s, d)
    for layer in range(DEPTH):
        kind, idx = layer % N_MIXERS, layer // N_MIXERS
        if kind == 0:
            x2d = _fox_layer(x2d, bsz, norm_w[layer], fox_w_in[idx], fox_b_f[idx], fox_q_gain[idx],
                             fox_k_gain[idx], fox_w_out[idx], str(idx))
        elif kind == 1:
            x2d = _gla_layer(x2d, bsz, norm_w[layer], gla_w_in[idx], gla_w_gate_up[idx], gla_b_gate[idx],
                             gla_o_gain[idx], gla_w_out[idx])
        else:
            x2d = _gdn_layer(x2d, bsz, norm_w[layer], gdn_w_in[idx], gdn_conv_w[idx], gdn_a_log[idx],
                             gdn_dt_bias[idx], gdn_o_gain[idx], gdn_w_out[idx])
    return x2d.reshape(bsz, s, d)
```

```python
import functools
import math

import jax
import jax.numpy as jnp
from jax import lax
from jax.experimental import pallas as pl
from jax.experimental.pallas import tpu as pltpu

F32 = jnp.float32
BF16 = jnp.bfloat16

D_MODEL = 1024
DEPTH = 4
N_MIXERS = 3
RMS_EPS = 1e-6
LANES = 128

FOX_HEADS = 8
FOX_HEAD_DIM = 128
FOX_WIDTH = 1024

GLA_HEADS = 4
GLA_KEY_DIM = 128
GLA_VAL_DIM = 256
GLA_WIDTH = 1024
GLA_RANK = 16
GLA_TAU = 16.0
GLA_CHUNK = 64

GDN_QK_HEADS = 4
GDN_V_HEADS = 8
GDN_HEAD_DIM = 128
GDN_WIDTH = 1024
GDN_CONV = 4
GDN_CHUNK = 64
GDN_CONV_DIM = 2048

VMEM_LIMIT_BYTES = 56 * 1024 * 1024

HIGHEST = lax.Precision.HIGHEST
LOG2_E = math.log2(math.e)

FOX_TQ, FOX_BQ, FOX_TK = 1024, 512, 512
FOX_BOUNDED_TQ = 2048
FOX_MAX_FIXED_SHIFT = 40.0
GLA_ROWS = 512
GDN_ROWS = 256


def _params(*semantics):
    return pltpu.CompilerParams(dimension_semantics=semantics, vmem_limit_bytes=VMEM_LIMIT_BYTES)


def _dot(a, b, precision=None):
    return jnp.dot(a, b, preferred_element_type=F32, precision=precision)


def _dot_nt(a, b):
    return lax.dot_general(a, b, (((1,), (1,)), ((), ())), preferred_element_type=F32)


def _sigmoid(x):
    return 1.0 / (1.0 + jnp.exp(-x))


def _silu(x):
    return x * _sigmoid(x)


def _softplus(x):
    return jnp.maximum(x, 0.0) + jnp.log(1.0 + jnp.exp(-jnp.abs(x)))


def _log_sigmoid(x):
    return -_softplus(-x)


def _chunk_cumsum(x, chunk):
    row = lax.broadcasted_iota(jnp.int32, x.shape, 0) & (chunk - 1)
    shift = 1
    while shift < chunk:
        x = x + jnp.where(row >= shift, pltpu.roll(x, shift, axis=0), 0.0)
        shift *= 2
    return x


def _inproj_kernel(*refs, tn, n_norm_groups, has_outproj, has_cumsum, tiles_per_seq):
    it = iter(refs)
    x_ref = next(it)
    g_ref, wo_ref = (next(it), next(it)) if has_outproj else (None, None)
    nw_ref, w_ref, we_ref, gain_ref = next(it), next(it), next(it), next(it)
    bias_ref, tri_ref = (next(it), next(it)) if has_cumsum else (None, None)
    xo_ref = next(it) if has_outproj else None
    o_ref = next(it)

    x = x_ref[...]
    if has_outproj:
        x = x + _dot(g_ref[...], wo_ref[...])
        xo_ref[...] = x
    ms = jnp.mean(x * x, axis=-1, keepdims=True)
    xn = (x * lax.rsqrt(ms + RMS_EPS) * nw_ref[...]).astype(BF16)
    extra = _dot(xn, we_ref[...])
    if has_cumsum:
        ccol_ref, crow_ref, carry_ref = next(it), next(it), next(it)

        @pl.when(pl.program_id(0) % tiles_per_seq == 0)
        def _():
            carry_ref[...] = jnp.zeros_like(carry_ref)

        log_f = _log_sigmoid(extra + bias_ref[...]) * LOG2_E
        cs = _dot(tri_ref[...], log_f, precision=HIGHEST) + carry_ref[...]
        tm = cs.shape[0]
        carry_ref[...] = cs[tm - 1:tm, :]
        ccol_ref[...] = cs
        crow_ref[0] = cs.T[:FOX_HEADS, :]
    else:
        next(it)[...] = extra
    n = o_ref.shape[1]
    for j in range(n // tn):
        acc = _dot(xn, w_ref[:, j * tn:(j + 1) * tn])
        for g in range(tn // LANES):
            col = j * tn + g * LANES
            seg = acc[:, g * LANES:(g + 1) * LANES]
            if col // LANES < n_norm_groups:
                msq = jnp.mean(seg * seg, axis=-1, keepdims=True)
                seg = seg * lax.rsqrt(msq + RMS_EPS) * gain_ref[:, col:col + LANES]
            o_ref[:, col:col + LANES] = seg.astype(o_ref.dtype)


def _inproj(x2d, prev, norm_w, w_main, w_ext, gain, n_norm_groups, out_dtype, name, forget_bias=None, seq_len=None):
    t, d = x2d.shape
    n = w_main.shape[1]
    tm, tn = 512, 512
    has_outproj = prev is not None
    has_cumsum = forget_bias is not None
    row = lambda i: (i, 0)
    fixed = lambda i: (0, 0)
    operands, in_specs = [x2d], [pl.BlockSpec((tm, d), row)]
    if has_outproj:
        g2d, w_out = prev
        operands += [g2d, w_out.astype(BF16)]
        in_specs += [pl.BlockSpec((tm, g2d.shape[1]), row), pl.BlockSpec(w_out.shape, fixed)]
    operands += [norm_w.reshape(1, d), w_main, w_ext, gain]
    in_specs += [pl.BlockSpec((1, d), fixed), pl.BlockSpec((d, n), fixed), pl.BlockSpec((d, LANES), fixed),
                 pl.BlockSpec((1, gain.shape[1]), fixed)]
    out_shape, out_specs, scratch = [], [], []
    if has_outproj:
        out_shape.append(jax.ShapeDtypeStruct((t, d), F32))
        out_specs.append(pl.BlockSpec((tm, d), row))
    out_shape.append(jax.ShapeDtypeStruct((t, n), out_dtype))
    out_specs.append(pl.BlockSpec((tm, n), row))
    out_shape.append(jax.ShapeDtypeStruct((t, LANES), F32))
    out_specs.append(pl.BlockSpec((tm, LANES), row))
    tiles_per_seq = None
    if has_cumsum:
        tiles_per_seq = seq_len // tm
        operands += [jnp.zeros((1, LANES), F32).at[0, :FOX_HEADS].set(forget_bias), jnp.tril(jnp.ones((tm, tm), F32))]
        in_specs += [pl.BlockSpec((1, LANES), fixed), pl.BlockSpec((tm, tm), fixed)]
        out_shape.append(jax.ShapeDtypeStruct((t // seq_len, FOX_HEADS, seq_len), F32))
        out_specs.append(pl.BlockSpec((1, FOX_HEADS, tm), lambda i: (i // tiles_per_seq, 0, i % tiles_per_seq)))
        scratch.append(pltpu.VMEM((1, LANES), F32))
    kern = functools.partial(_inproj_kernel, tn=tn, n_norm_groups=n_norm_groups, has_outproj=has_outproj,
                             has_cumsum=has_cumsum, tiles_per_seq=tiles_per_seq)
    outs = pl.pallas_call(
        kern,
        grid=(t // tm,),
        in_specs=in_specs,
        out_specs=out_specs,
        out_shape=out_shape,
        scratch_shapes=scratch,
        compiler_params=_params("arbitrary" if has_cumsum else "parallel"),
        name=name,
    )(*operands)
    x_out = outs[0] if has_outproj else x2d
    rest = outs[1:] if has_outproj else outs
    return x_out, rest[0], (tuple(rest[1:]) if has_cumsum else rest[1])


def _split_w_in(w_in, n_main):
    d, n = w_in.shape
    w_main = w_in[:, :n_main].astype(BF16)
    w_ext = jnp.zeros((d, LANES), F32).at[:, :n - n_main].set(w_in[:, n_main:]).astype(BF16)
    return w_main, w_ext


def _outproj_kernel(g_ref, w_ref, x_ref, o_ref):
    o_ref[...] = x_ref[...] + _dot(g_ref[...], w_ref[...])


def _outproj(g2d, w_out, x2d, name):
    t, d = x2d.shape
    k = g2d.shape[1]
    tm = 512
    return pl.pallas_call(
        _outproj_kernel,
        grid=(t // tm,),
        in_specs=[
            pl.BlockSpec((tm, k), lambda i: (i, 0)),
            pl.BlockSpec((k, d), lambda i: (0, 0)),
            pl.BlockSpec((tm, d), lambda i: (i, 0)),
        ],
        out_specs=pl.BlockSpec((tm, d), lambda i: (i, 0)),
        out_shape=jax.ShapeDtypeStruct((t, d), F32),
        compiler_params=_params("parallel"),
        name=name,
    )(g2d, w_out.astype(BF16), x2d)


def _fox_attn_kernel(q_ref, k_ref, v_ref, z_ref, ccol_ref, crow_ref, o_ref, *, tq, bq, tk):
    h = pl.program_id(1)
    i = pl.program_id(2)
    nsub = tq // bq
    lane = lax.broadcasted_iota(jnp.int32, (bq, LANES), 1)
    qs = [q_ref[0, r * bq:(r + 1) * bq, :] for r in range(nsub)]
    cqs = [jnp.sum(jnp.where(lane == h, ccol_ref[0, r * bq:(r + 1) * bq, :], 0.0), axis=-1, keepdims=True)
           for r in range(nsub)]
    row = lax.broadcasted_iota(jnp.int32, (bq, tk), 0)
    col = lax.broadcasted_iota(jnp.int32, (bq, tk), 1)

    def load_keys(start):
        return (k_ref[0, pl.ds(start, tk), :], v_ref[0, pl.ds(start, tk), :], crow_ref[0, :, pl.ds(start, tk)])

    def attend(r, carry, keys, row_minus_col):
        m, l, acc = carry
        k, v, ck = keys
        t = _dot_nt(qs[r], k) - ck
        if row_minus_col is not None:
            t = jnp.where(row + row_minus_col >= col, t, -jnp.inf)
        m_new = jnp.maximum(m, jnp.max(t, axis=-1, keepdims=True) + cqs[r])
        alpha = jnp.exp2(m - m_new)
        p = jnp.exp2(t + (cqs[r] - m_new))
        l = alpha * l + jnp.sum(p, axis=-1, keepdims=True)
        acc = alpha * acc + _dot(p.astype(BF16), v)
        return m_new, l, acc

    def below_diagonal(kb, carries):
        keys = load_keys(pl.multiple_of(kb * tk, tk))
        return tuple(attend(r, carries[r], keys, None) for r in range(nsub))

    init = tuple((jnp.full((bq, 1), -jnp.inf, F32), jnp.zeros((bq, 1), F32), jnp.zeros((bq, LANES), F32))
                 for _ in range(nsub))
    carries = list(lax.fori_loop(0, i * (tq // tk), below_diagonal, init))
    for c in range(tq // tk):
        keys = load_keys(pl.multiple_of(i * tq + c * tk, tk))
        for r in range(nsub):
            row0, col0 = r * bq, c * tk
            if col0 > row0 + bq - 1:
                continue
            crosses = col0 + tk - 1 > row0
            carries[r] = attend(r, carries[r], keys, (row0 - col0) if crosses else None)
    for r in range(nsub):
        _, l, acc = carries[r]
        z = z_ref[0, r * bq:(r + 1) * bq, :].astype(F32)
        o_ref[0, r * bq:(r + 1) * bq, :] = (acc / l * _silu(z)).astype(o_ref.dtype)


def _fox_attn_bounded_kernel(q_ref, k_ref, v_ref, z_ref, ccol_ref, crow_ref, bound_ref, o_ref, *, tq, tk, unroll):
    h = pl.program_id(1)
    i = pl.program_id(2)
    nsub = tq // tk
    lane = lax.broadcasted_iota(jnp.int32, (tq, LANES), 1)
    cq = jnp.sum(jnp.where(lane == h, ccol_ref[0], 0.0), axis=-1, keepdims=True)
    dq = cq - bound_ref[0:1, 0:1]
    q = q_ref[0]
    ones = jnp.ones((tk, LANES), BF16)

    def weighted_values(start, first_row, diagonal):
        k = k_ref[0, pl.ds(start, tk), :]
        v1 = jnp.concatenate([v_ref[0, pl.ds(start, tk), :], ones], axis=1)
        ck = crow_ref[0, :, pl.ds(start, tk)]
        e = (_dot_nt(q[first_row:], k) - ck) + dq[first_row:]
        if diagonal:
            row = lax.broadcasted_iota(jnp.int32, (tk, tk), 0)
            col = lax.broadcasted_iota(jnp.int32, (tk, tk), 1)
            top = jnp.where(row >= col, e[:tk], -jnp.inf)
            e = top if e.shape[0] == tk else jnp.concatenate([top, e[tk:]], axis=0)
        return _dot(jnp.exp2(e).astype(BF16), v1)

    def below_diagonal(jj, acc):
        for u in range(unroll):
            acc = acc + weighted_values(pl.multiple_of((jj * unroll + u) * tk, tk), 0, False)
        return acc

    acc = lax.fori_loop(0, i * nsub // unroll, below_diagonal, jnp.zeros((tq, 2 * LANES), F32))
    for c in range(nsub):
        pv = weighted_values(pl.multiple_of(i * tq + c * tk, tk), c * tk, True)
        acc = acc + pv if c == 0 else jnp.concatenate([acc[:c * tk], acc[c * tk:] + pv], axis=0)
    z = z_ref[0].astype(F32)
    o_ref[0] = (acc[:, :LANES] / acc[:, LANES:LANES + 1] * _silu(z)).astype(o_ref.dtype)


def _fox_attn(y, ccol, crow, logit_bound):
    b, s, _ = y.shape
    h = FOX_HEADS
    crow3 = crow.reshape(b * h, 1, s)

    def specs(tq):
        return [
            pl.BlockSpec((1, tq, LANES), lambda bi, hi, i: (bi, i, hi)),
            pl.BlockSpec((1, s, LANES), lambda bi, hi, i: (bi, 0, h + hi)),
            pl.BlockSpec((1, s, LANES), lambda bi, hi, i: (bi, 0, 2 * h + hi)),
            pl.BlockSpec((1, tq, LANES), lambda bi, hi, i: (bi, i, 3 * h + hi)),
            pl.BlockSpec((1, tq, LANES), lambda bi, hi, i: (bi, i, 0)),
            pl.BlockSpec((1, 1, s), lambda bi, hi, i: (bi * h + hi, 0, 0)),
        ]

    def online():
        tq, bq, tk = min(FOX_TQ, s), FOX_BQ, FOX_TK
        return pl.pallas_call(
            functools.partial(_fox_attn_kernel, tq=tq, bq=bq, tk=tk),
            grid=(b, h, s // tq),
            in_specs=specs(tq),
            out_specs=pl.BlockSpec((1, tq, LANES), lambda bi, hi, i: (bi, i, hi)),
            out_shape=jax.ShapeDtypeStruct((b, s, FOX_WIDTH), BF16),
            compiler_params=_params("parallel", "parallel", "arbitrary"),
            name="fox_attn",
        )(y, y, y, y, ccol, crow3)

    def bounded():
        tq, tk = min(FOX_BOUNDED_TQ, s), FOX_TK
        unroll = tq // tk
        return pl.pallas_call(
            functools.partial(_fox_attn_bounded_kernel, tq=tq, tk=tk, unroll=unroll),
            grid=(b, h, s // tq),
            in_specs=specs(tq) + [pl.BlockSpec((1, LANES), lambda bi, hi, i: (0, 0))],
            out_specs=pl.BlockSpec((1, tq, LANES), lambda bi, hi, i: (bi, i, hi)),
            out_shape=jax.ShapeDtypeStruct((b, s, FOX_WIDTH), BF16),
            compiler_params=_params("parallel", "parallel", "arbitrary"),
            name="fox_attn_bounded",
        )(y, y, y, y, ccol, crow3, jnp.full((1, LANES), logit_bound, F32))

    return lax.cond(logit_bound <= FOX_MAX_FIXED_SHIFT, bounded, online)


def _fox_layer(x2d, prev, bsz, norm_w, w_in, b_f, q_gain, k_gain, tag):
    t, d = x2d.shape
    s = t // bsz
    w_main, w_ext = _split_w_in(w_in, 4 * FOX_WIDTH)
    scale = FOX_HEAD_DIM ** -0.5 * LOG2_E
    gain = jnp.concatenate([jnp.tile(q_gain * scale, FOX_HEADS), jnp.tile(k_gain, FOX_HEADS)]).reshape(1, -1)
    x2d, y, (ccol, crow) = _inproj(x2d, prev, norm_w, w_main, w_ext, gain, 2 * FOX_HEADS, BF16, "fox_front" + tag,
                                   forget_bias=b_f, seq_len=s)
    logit_bound = FOX_HEAD_DIM * jnp.max(jnp.abs(q_gain * scale)) * jnp.max(jnp.abs(k_gain)) * 1.02
    g = _fox_attn(y.reshape(bsz, s, -1), ccol.reshape(bsz, s, LANES), crow, logit_bound)
    return x2d, g.reshape(t, FOX_WIDTH)


def _gla_kernel(q_ref, k_ref, v_ref, z_ref, ext_ref, wup_ref, bg_ref, gain_ref, o_ref, state_ref, *, rows):
    chunk = GLA_CHUNK
    chunks = [slice(c * chunk, (c + 1) * chunk) for c in range(rows // chunk)]

    @pl.when(pl.program_id(2) == 0)
    def _():
        state_ref[...] = jnp.zeros_like(state_ref)

    gate = _dot(ext_ref[0].astype(BF16), wup_ref[...]) + bg_ref[...]
    log_a = _log_sigmoid(gate) * (1.0 / GLA_TAU)
    bcum = _chunk_cumsum(log_a, chunk)
    q = q_ref[0].astype(F32) * (GLA_KEY_DIM ** -0.5)
    k = k_ref[0].astype(F32)
    q_dec = (q * jnp.exp(bcum)).astype(BF16)
    k_inv = (k * jnp.exp(-bcum)).astype(BF16)
    row = lax.broadcasted_iota(jnp.int32, (chunk, chunk), 0)
    col = lax.broadcasted_iota(jnp.int32, (chunk, chunk), 1)
    causal = row >= col
    b_last = [bcum[sl][chunk - 1:chunk, :] for sl in chunks]
    k_dec = [(k[sl] * jnp.exp(b_last[c] - bcum[sl])).astype(BF16) for c, sl in enumerate(chunks)]
    v16 = [v_ref[0, sl, :] for sl in chunks]
    attn = [jnp.where(causal, _dot_nt(q_dec[sl], k_inv[sl]), 0.0).astype(BF16) for sl in chunks]
    update = [_dot(v16[c].astype(F32).T.astype(BF16), k_dec[c]) for c in range(len(chunks))]
    state_t = state_ref[...]
    s16 = []
    for c in range(len(chunks)):
        s16.append(state_t.astype(BF16))
        state_t = state_t * jnp.exp(b_last[c]) + update[c]
    state_ref[...] = state_t
    for c, sl in enumerate(chunks):
        o = _dot(attn[c], v16[c]) + _dot_nt(q_dec[sl], s16[c])
        ms = jnp.mean(o * o, axis=-1, keepdims=True)
        on = o * lax.rsqrt(ms + RMS_EPS) * gain_ref[...]
        o_ref[0, sl, :] = (on * _silu(z_ref[0, sl, :].astype(F32))).astype(o_ref.dtype)


def _gla_layer(x2d, prev, bsz, norm_w, w_in, w_gate_up, b_gate, o_gain):
    t, d = x2d.shape
    s = t // bsz
    h, dk, dv = GLA_HEADS, GLA_KEY_DIM, GLA_VAL_DIM
    n_main = 2 * h * dk + 2 * GLA_WIDTH
    w_main, w_ext = _split_w_in(w_in, n_main)
    x2d, y, ext = _inproj(x2d, prev, norm_w, w_main, w_ext, jnp.ones((1, LANES), F32), 0, BF16, "gla_front")
    y = y.reshape(bsz, s, n_main)
    ext = ext.reshape(bsz, s, LANES)
    wup = jnp.zeros((LANES, h * dk), F32).at[:GLA_RANK].set(w_gate_up).astype(BF16)
    rows = min(GLA_ROWS, s)
    kern = functools.partial(_gla_kernel, rows=rows)
    k_off = h * dk // dk
    v_off = 2 * h * dk // dv
    z_off = (2 * h * dk + GLA_WIDTH) // dv
    g = pl.pallas_call(
        kern,
        grid=(bsz, h, s // rows),
        in_specs=[
            pl.BlockSpec((1, rows, dk), lambda bi, hi, i: (bi, i, hi)),
            pl.BlockSpec((1, rows, dk), lambda bi, hi, i: (bi, i, k_off + hi)),
            pl.BlockSpec((1, rows, dv), lambda bi, hi, i: (bi, i, v_off + hi)),
            pl.BlockSpec((1, rows, dv), lambda bi, hi, i: (bi, i, z_off + hi)),
            pl.BlockSpec((1, rows, LANES), lambda bi, hi, i: (bi, i, 0)),
            pl.BlockSpec((LANES, dk), lambda bi, hi, i: (0, hi)),
            pl.BlockSpec((1, dk), lambda bi, hi, i: (0, hi)),
            pl.BlockSpec((1, dv), lambda bi, hi, i: (0, 0)),
        ],
        out_specs=pl.BlockSpec((1, rows, dv), lambda bi, hi, i: (bi, i, hi)),
        out_shape=jax.ShapeDtypeStruct((bsz, s, GLA_WIDTH), BF16),
        scratch_shapes=[pltpu.VMEM((dv, dk), F32)],
        compiler_params=_params("parallel", "parallel", "arbitrary"),
        name="gla_chunk",
    )(y, y, y, y, ext, wup, b_gate.reshape(1, -1), o_gain.reshape(1, -1))
    return x2d, g.reshape(t, GLA_WIDTH)


GDN_HISTORY = 8


def _gdn_conv_heads(u_ref, w_ref, hist_ref, rows):
    pad = GDN_HISTORY
    first = pl.program_id(1) == 0

    @pl.when(first)
    def _():
        hist_ref[0:pad, :] = jnp.zeros((pad, hist_ref.shape[1]), F32)

    @pl.when(jnp.logical_not(first))
    def _():
        hist_ref[0:pad, :] = hist_ref[rows:rows + pad, :]

    hist_ref[pad:pad + rows, :] = u_ref[0].astype(F32)
    heads = []
    for g in range(GDN_CONV_DIM // LANES):
        cols = slice(g * LANES, (g + 1) * LANES)
        acc = jnp.zeros((rows, LANES), F32)
        for tap in range(GDN_CONV):
            off = pad - (GDN_CONV - 1) + tap
            acc = acc + hist_ref[off:off + rows, cols] * w_ref[tap:tap + 1, cols]
        seg = _silu(acc)
        if g < 2 * GDN_QK_HEADS:
            ss = jnp.sum(seg * seg, axis=-1, keepdims=True)
            seg = seg * lax.rsqrt(ss + RMS_EPS)
            if g < GDN_QK_HEADS:
                seg = seg * (GDN_HEAD_DIM ** -0.5)
        heads.append(seg)
    n_hq = GDN_QK_HEADS
    return heads[:n_hq], heads[n_hq:2 * n_hq], heads[2 * n_hq:]


def _gdn_kernel(u_ref, z_ref, ext_ref, cw_ref, alog_ref, dtb_ref, gain_ref, o_ref, state_ref, hist_ref, *, rows):
    chunk = GDN_CHUNK
    d = GDN_HEAD_DIM
    n_hq, n_hv = GDN_QK_HEADS, GDN_V_HEADS
    rep = n_hv // n_hq
    chunks = [slice(c * chunk, (c + 1) * chunk) for c in range(rows // chunk)]
    pairs = [(hv, c) for hv in range(n_hv) for c in range(len(chunks))]

    @pl.when(pl.program_id(1) == 0)
    def _():
        state_ref[...] = jnp.zeros_like(state_ref)

    q, k, v_heads = _gdn_conv_heads(u_ref, cw_ref, hist_ref, rows)

    ext = ext_ref[0]
    g_all = -jnp.exp(alog_ref[...]) * _softplus(ext + dtb_ref[...])
    gc_all = _chunk_cumsum(g_all, chunk)
    gc_all_t = gc_all.T
    beta_all = _sigmoid(ext)
    row = lax.broadcasted_iota(jnp.int32, (chunk, chunk), 0)
    col = lax.broadcasted_iota(jnp.int32, (chunk, chunk), 1)
    causal = row >= col
    strict = row > col

    q16 = [x.astype(BF16) for x in q]
    k16 = [x.astype(BF16) for x in k]
    kk = {(h, c): _dot_nt(k16[h][sl], k16[h][sl]) for h in range(n_hq) for c, sl in enumerate(chunks)}
    qk = {(h, c): _dot_nt(q16[h][sl], k16[h][sl]) for h in range(n_hq) for c, sl in enumerate(chunks)}

    gc, rhs, q_dec = [], [], []
    for hv in range(n_hv):
        h = hv // rep
        gc_h = gc_all[:, hv:hv + 1]
        beta = beta_all[:, n_hv + hv:n_hv + hv + 1]
        eg = jnp.exp(gc_h)
        v = v_heads[hv]
        gc.append(gc_h)
        rhs.append(jnp.concatenate([v * beta, k[h] * (beta * eg)], axis=-1))
        q_dec.append((q[h] * eg).astype(BF16))

    decay, nk, m = {}, {}, {}
    for hv, c in pairs:
        sl = chunks[c]
        diff = gc[hv][sl] - gc_all_t[hv:hv + 1, sl]
        decay[hv, c] = jnp.where(causal, jnp.exp(jnp.where(causal, diff, 0.0)), 0.0)
        beta = beta_all[sl, n_hv + hv:n_hv + hv + 1]
        nk[hv, c] = -jnp.where(strict, kk[hv // rep, c] * beta * decay[hv, c], 0.0)
        m[hv, c] = nk[hv, c]
    span = 2
    while span < chunk:
        n16 = {p: nk[p].astype(BF16) for p in pairs}
        nk = {p: _dot(n16[p], n16[p]) for p in pairs}
        mn = {p: _dot(m[p].astype(BF16), nk[p].astype(BF16)) for p in pairs}
        m = {p: m[p] + nk[p] + mn[p] for p in pairs}
        span *= 2
    uw = {(hv, c): rhs[hv][chunks[c]] + _dot(m[hv, c].astype(BF16), rhs[hv][chunks[c]].astype(BF16))
          for hv, c in pairs}

    heads = range(n_hv)
    state = [state_ref[hv] for hv in heads]
    for c, sl in enumerate(chunks):
        g_last = [gc[hv][sl][chunk - 1:chunk, :] for hv in heads]
        k_dec_t = [(k[hv // rep][sl] * jnp.exp(g_last[hv] - gc[hv][sl])).T.astype(BF16) for hv in heads]
        qk16 = [(qk[hv // rep, c] * decay[hv, c]).astype(BF16) for hv in heads]
        s16 = [state[hv].astype(BF16) for hv in heads]
        ws = [_dot(uw[hv, c][:, d:].astype(BF16), s16[hv]) for hv in heads]
        qs = [_dot(q_dec[hv][sl], s16[hv]) for hv in heads]
        v16 = [(uw[hv, c][:, :d] - ws[hv]).astype(BF16) for hv in heads]
        o = [qs[hv] + _dot(qk16[hv], v16[hv]) for hv in heads]
        state = [state[hv] * jnp.exp(g_last[hv]) + _dot(k_dec_t[hv], v16[hv]) for hv in heads]
        for hv in heads:
            ms = jnp.mean(o[hv] * o[hv], axis=-1, keepdims=True)
            on = o[hv] * lax.rsqrt(ms + RMS_EPS) * gain_ref[...]
            z = z_ref[0, sl, hv * d:(hv + 1) * d].astype(F32)
            o_ref[0, sl, hv * d:(hv + 1) * d] = (on * _silu(z)).astype(o_ref.dtype)
    for hv in heads:
        state_ref[hv] = state[hv]


def _gdn_layer(x2d, prev, bsz, norm_w, w_in, conv_w, a_log, dt_bias, o_gain):
    t, dm = x2d.shape
    s = t // bsz
    hv, d = GDN_V_HEADS, GDN_HEAD_DIM
    n_main = GDN_CONV_DIM + GDN_WIDTH
    w_main, w_ext = _split_w_in(w_in, n_main)
    x2d, y, ext = _inproj(x2d, prev, norm_w, w_main, w_ext, jnp.ones((1, LANES), F32), 0, BF16, "gdn_front")
    y = y.reshape(bsz, s, n_main)
    ext = ext.reshape(bsz, s, LANES)
    alog = jnp.zeros((1, LANES), F32).at[0, :hv].set(a_log)
    dtb = jnp.zeros((1, LANES), F32).at[0, :hv].set(dt_bias)
    rows = min(GDN_ROWS, s)
    kern = functools.partial(_gdn_kernel, rows=rows)
    g = pl.pallas_call(
        kern,
        grid=(bsz, s // rows),
        in_specs=[
            pl.BlockSpec((1, rows, GDN_CONV_DIM), lambda bi, i: (bi, i, 0)),
            pl.BlockSpec((1, rows, GDN_WIDTH), lambda bi, i: (bi, i, GDN_CONV_DIM // GDN_WIDTH)),
            pl.BlockSpec((1, rows, LANES), lambda bi, i: (bi, i, 0)),
            pl.BlockSpec((GDN_CONV, GDN_CONV_DIM), lambda bi, i: (0, 0)),
            pl.BlockSpec((1, LANES), lambda bi, i: (0, 0)),
            pl.BlockSpec((1, LANES), lambda bi, i: (0, 0)),
            pl.BlockSpec((1, d), lambda bi, i: (0, 0)),
        ],
        out_specs=pl.BlockSpec((1, rows, GDN_WIDTH), lambda bi, i: (bi, i, 0)),
        out_shape=jax.ShapeDtypeStruct((bsz, s, GDN_WIDTH), BF16),
        scratch_shapes=[pltpu.VMEM((hv, d, d), F32), pltpu.VMEM((GDN_HISTORY + rows, GDN_CONV_DIM), F32)],
        compiler_params=_params("parallel", "arbitrary"),
        name="gdn_chunk",
    )(y, y, ext, conv_w, alog, dtb, o_gain.reshape(1, -1))
    return x2d, g.reshape(t, GDN_WIDTH)


def kernel(x, norm_w, fox_w_in, fox_b_f, fox_q_gain, fox_k_gain, fox_w_out, gla_w_in, gla_w_gate_up, gla_b_gate, gla_o_gain, gla_w_out, gdn_w_in, gdn_conv_w, gdn_a_log, gdn_dt_bias, gdn_o_gain, gdn_w_out):
    bsz, s, d = x.shape
    x2d = x.reshape(bsz * s, d)
    prev = None
    for layer in range(DEPTH):
        kind, idx = layer % N_MIXERS, layer // N_MIXERS
        if kind == 0:
            x2d, g = _fox_layer(x2d, prev, bsz, norm_w[layer], fox_w_in[idx], fox_b_f[idx], fox_q_gain[idx],
                                fox_k_gain[idx], str(idx))
            prev = (g, fox_w_out[idx])
        elif kind == 1:
            x2d, g = _gla_layer(x2d, prev, bsz, norm_w[layer], gla_w_in[idx], gla_w_gate_up[idx], gla_b_gate[idx],
                                gla_o_gain[idx])
            prev = (g, gla_w_out[idx])
        else:
            x2d, g = _gdn_layer(x2d, prev, bsz, norm_w[layer], gdn_w_in[idx], gdn_conv_w[idx], gdn_a_log[idx],
                                gdn_dt_bias[idx], gdn_o_gain[idx])
            prev = (g, gdn_w_out[idx])
    x2d = _outproj(prev[0], prev[1], x2d, "outproj_last")
    return x2d.reshape(bsz, s, d)
```

```python
import functools
import math

import jax
import jax.numpy as jnp
from jax import lax
from jax.experimental import pallas as pl
from jax.experimental.pallas import tpu as pltpu

F32 = jnp.float32
BF16 = jnp.bfloat16

D_MODEL = 1024
DEPTH = 4
N_MIXERS = 3
RMS_EPS = 1e-6
LANES = 128

FOX_HEADS = 8
FOX_HEAD_DIM = 128
FOX_WIDTH = 1024

GLA_HEADS = 4
GLA_KEY_DIM = 128
GLA_VAL_DIM = 256
GLA_WIDTH = 1024
GLA_RANK = 16
GLA_TAU = 16.0
GLA_CHUNK = 64

GDN_QK_HEADS = 4
GDN_V_HEADS = 8
GDN_HEAD_DIM = 128
GDN_WIDTH = 1024
GDN_CONV = 4
GDN_CHUNK = 64
GDN_CONV_DIM = 2048

VMEM_LIMIT_BYTES = 56 * 1024 * 1024

LOG2_E = math.log2(math.e)

FOX_TQ, FOX_BQ, FOX_TK = 1024, 512, 512
FOX_BOUNDED_TQ = 2048
FOX_MAX_FIXED_SHIFT = 40.0
GLA_ROWS = 512
GDN_ROWS = 256


def _params(*semantics):
    return pltpu.CompilerParams(dimension_semantics=semantics, vmem_limit_bytes=VMEM_LIMIT_BYTES)


def _dot(a, b):
    return jnp.dot(a, b, preferred_element_type=F32)


def _dot_nt(a, b):
    return lax.dot_general(a, b, (((1,), (1,)), ((), ())), preferred_element_type=F32)


def _sigmoid(x):
    return 1.0 / (1.0 + jnp.exp2(x * (-LOG2_E)))


def _silu(x):
    return x * _sigmoid(x)


def _softplus(x):
    return jnp.maximum(x, 0.0) + jnp.log(1.0 + jnp.exp(-jnp.abs(x)))


def _log_sigmoid(x):
    return -_softplus(-x)


def _chunk_cumsum(x, chunk):
    row = lax.broadcasted_iota(jnp.int32, x.shape, 0) & (chunk - 1)
    shift = 1
    while shift < chunk:
        x = x + jnp.where(row >= shift, pltpu.roll(x, shift, axis=0), 0.0)
        shift *= 2
    return x


def _inproj_kernel(*refs, tn, n_norm_groups, has_outproj, has_cumsum, tiles_per_seq):
    it = iter(refs)
    x_ref = next(it)
    g_ref, wo_ref = (next(it), next(it)) if has_outproj else (None, None)
    nw_ref, w_ref, we_ref, gain_ref = next(it), next(it), next(it), next(it)
    bias_ref = next(it) if has_cumsum else None
    xo_ref = next(it) if has_outproj else None
    o_ref = next(it)

    tm = x_ref.shape[0]
    xn = []
    for r in (slice(0, tm // 2), slice(tm // 2, tm)):
        x = x_ref[r, :]
        if has_outproj:
            x = x + _dot(g_ref[r, :], wo_ref[...])
            xo_ref[r, :] = x
        ms = jnp.mean(x * x, axis=-1, keepdims=True)
        xn.append((x * lax.rsqrt(ms + RMS_EPS) * nw_ref[...]).astype(BF16))
    xn = jnp.concatenate(xn, axis=0)
    extra = _dot(xn, we_ref[...])
    if has_cumsum:
        ccol_ref, crow_ref, carry_ref = next(it), next(it), next(it)

        @pl.when(pl.program_id(0) % tiles_per_seq == 0)
        def _():
            carry_ref[...] = jnp.zeros_like(carry_ref)

        log_f = _log_sigmoid(extra + bias_ref[...]) * LOG2_E
        tm = log_f.shape[0]
        cs = _chunk_cumsum(log_f, tm) + carry_ref[...]
        carry_ref[...] = cs[tm - 1:tm, :]
        ccol_ref[...] = cs
        crow_ref[0] = cs.T[:FOX_HEADS, :]
    else:
        next(it)[...] = extra
    n = o_ref.shape[1]
    for j in range(n // tn):
        acc = _dot(xn, w_ref[:, j * tn:(j + 1) * tn])
        for g in range(tn // LANES):
            col = j * tn + g * LANES
            seg = acc[:, g * LANES:(g + 1) * LANES]
            if col // LANES < n_norm_groups:
                msq = jnp.mean(seg * seg, axis=-1, keepdims=True)
                seg = seg * lax.rsqrt(msq + RMS_EPS) * gain_ref[:, col:col + LANES]
            o_ref[:, col:col + LANES] = seg.astype(o_ref.dtype)


def _inproj(x2d, prev, norm_w, w_main, w_ext, gain, n_norm_groups, out_dtype, name, forget_bias=None, seq_len=None):
    t, d = x2d.shape
    n = w_main.shape[1]
    tm, tn = 512, 512
    has_outproj = prev is not None
    has_cumsum = forget_bias is not None
    row = lambda i: (i, 0)
    fixed = lambda i: (0, 0)
    operands, in_specs = [x2d], [pl.BlockSpec((tm, d), row)]
    if has_outproj:
        g2d, w_out = prev
        operands += [g2d, w_out.astype(BF16)]
        in_specs += [pl.BlockSpec((tm, g2d.shape[1]), row), pl.BlockSpec(w_out.shape, fixed)]
    operands += [norm_w.reshape(1, d), w_main, w_ext, gain]
    in_specs += [pl.BlockSpec((1, d), fixed), pl.BlockSpec((d, n), fixed), pl.BlockSpec((d, LANES), fixed),
                 pl.BlockSpec((1, gain.shape[1]), fixed)]
    out_shape, out_specs, scratch = [], [], []
    if has_outproj:
        out_shape.append(jax.ShapeDtypeStruct((t, d), F32))
        out_specs.append(pl.BlockSpec((tm, d), row))
    out_shape.append(jax.ShapeDtypeStruct((t, n), out_dtype))
    out_specs.append(pl.BlockSpec((tm, n), row))
    out_shape.append(jax.ShapeDtypeStruct((t, LANES), F32))
    out_specs.append(pl.BlockSpec((tm, LANES), row))
    tiles_per_seq = None
    if has_cumsum:
        tiles_per_seq = seq_len // tm
        operands += [jnp.zeros((1, LANES), F32).at[0, :FOX_HEADS].set(forget_bias)]
        in_specs += [pl.BlockSpec((1, LANES), fixed)]
        out_shape.append(jax.ShapeDtypeStruct((t // seq_len, FOX_HEADS, seq_len), F32))
        out_specs.append(pl.BlockSpec((1, FOX_HEADS, tm), lambda i: (i // tiles_per_seq, 0, i % tiles_per_seq)))
        scratch.append(pltpu.VMEM((1, LANES), F32))
    kern = functools.partial(_inproj_kernel, tn=tn, n_norm_groups=n_norm_groups, has_outproj=has_outproj,
                             has_cumsum=has_cumsum, tiles_per_seq=tiles_per_seq)
    outs = pl.pallas_call(
        kern,
        grid=(t // tm,),
        in_specs=in_specs,
        out_specs=out_specs,
        out_shape=out_shape,
        scratch_shapes=scratch,
        compiler_params=_params("arbitrary" if has_cumsum else "parallel"),
        name=name,
    )(*operands)
    x_out = outs[0] if has_outproj else x2d
    rest = outs[1:] if has_outproj else outs
    return x_out, rest[0], (tuple(rest[1:]) if has_cumsum else rest[1])


def _split_w_in(w_in, n_main):
    d, n = w_in.shape
    w_main = w_in[:, :n_main].astype(BF16)
    w_ext = jnp.zeros((d, LANES), F32).at[:, :n - n_main].set(w_in[:, n_main:]).astype(BF16)
    return w_main, w_ext


def _outproj_kernel(g_ref, w_ref, x_ref, o_ref):
    o_ref[...] = x_ref[...] + _dot(g_ref[...], w_ref[...])


def _outproj(g2d, w_out, x2d, name):
    t, d = x2d.shape
    k = g2d.shape[1]
    tm = 512
    return pl.pallas_call(
        _outproj_kernel,
        grid=(t // tm,),
        in_specs=[
            pl.BlockSpec((tm, k), lambda i: (i, 0)),
            pl.BlockSpec((k, d), lambda i: (0, 0)),
            pl.BlockSpec((tm, d), lambda i: (i, 0)),
        ],
        out_specs=pl.BlockSpec((tm, d), lambda i: (i, 0)),
        out_shape=jax.ShapeDtypeStruct((t, d), F32),
        compiler_params=_params("parallel"),
        name=name,
    )(g2d, w_out.astype(BF16), x2d)


def _fox_attn_kernel(q_ref, k_ref, v_ref, z_ref, ccol_ref, crow_ref, o_ref, *, tq, bq, tk):
    h = pl.program_id(1)
    i = pl.program_id(2)
    nsub = tq // bq
    lane = lax.broadcasted_iota(jnp.int32, (bq, LANES), 1)
    qs = [q_ref[0, r * bq:(r + 1) * bq, :] for r in range(nsub)]
    cqs = [jnp.sum(jnp.where(lane == h, ccol_ref[0, r * bq:(r + 1) * bq, :], 0.0), axis=-1, keepdims=True)
           for r in range(nsub)]
    row = lax.broadcasted_iota(jnp.int32, (bq, tk), 0)
    col = lax.broadcasted_iota(jnp.int32, (bq, tk), 1)

    def load_keys(start):
        return (k_ref[0, pl.ds(start, tk), :], v_ref[0, pl.ds(start, tk), :], crow_ref[0, :, pl.ds(start, tk)])

    def attend(r, carry, keys, row_minus_col):
        m, l, acc = carry
        k, v, ck = keys
        t = _dot_nt(qs[r], k) - ck
        if row_minus_col is not None:
            t = jnp.where(row + row_minus_col >= col, t, -jnp.inf)
        m_new = jnp.maximum(m, jnp.max(t, axis=-1, keepdims=True) + cqs[r])
        alpha = jnp.exp2(m - m_new)
        p = jnp.exp2(t + (cqs[r] - m_new))
        l = alpha * l + jnp.sum(p, axis=-1, keepdims=True)
        acc = alpha * acc + _dot(p.astype(BF16), v)
        return m_new, l, acc

    def below_diagonal(kb, carries):
        keys = load_keys(pl.multiple_of(kb * tk, tk))
        return tuple(attend(r, carries[r], keys, None) for r in range(nsub))

    init = tuple((jnp.full((bq, 1), -jnp.inf, F32), jnp.zeros((bq, 1), F32), jnp.zeros((bq, LANES), F32))
                 for _ in range(nsub))
    carries = list(lax.fori_loop(0, i * (tq // tk), below_diagonal, init))
    for c in range(tq // tk):
        keys = load_keys(pl.multiple_of(i * tq + c * tk, tk))
        for r in range(nsub):
            row0, col0 = r * bq, c * tk
            if col0 > row0 + bq - 1:
                continue
            crosses = col0 + tk - 1 > row0
            carries[r] = attend(r, carries[r], keys, (row0 - col0) if crosses else None)
    for r in range(nsub):
        _, l, acc = carries[r]
        z = z_ref[0, r * bq:(r + 1) * bq, :].astype(F32)
        o_ref[0, r * bq:(r + 1) * bq, :] = (acc / l * _silu(z)).astype(o_ref.dtype)


def _fox_attn_bounded_kernel(q_ref, k_ref, v_ref, z_ref, ccol_ref, crow_ref, bound_ref, o_ref, *, tq, tk, unroll):
    h = pl.program_id(1)
    i = pl.program_id(2)
    nsub = tq // tk
    lane = lax.broadcasted_iota(jnp.int32, (tq, LANES), 1)
    cq = jnp.sum(jnp.where(lane == h, ccol_ref[0], 0.0), axis=-1, keepdims=True)
    dq = cq - bound_ref[0:1, 0:1]
    q = q_ref[0]
    ones = jnp.ones((tk, LANES), BF16)

    def weighted_values(start, first_row, diagonal):
        k = k_ref[0, pl.ds(start, tk), :]
        v1 = jnp.concatenate([v_ref[0, pl.ds(start, tk), :], ones], axis=1)
        ck = crow_ref[0, :, pl.ds(start, tk)]
        e = (_dot_nt(q[first_row:], k) - ck) + dq[first_row:]
        if diagonal:
            row = lax.broadcasted_iota(jnp.int32, (tk, tk), 0)
            col = lax.broadcasted_iota(jnp.int32, (tk, tk), 1)
            top = jnp.where(row >= col, e[:tk], -jnp.inf)
            e = top if e.shape[0] == tk else jnp.concatenate([top, e[tk:]], axis=0)
        return _dot(jnp.exp2(e).astype(BF16), v1)

    def below_diagonal(jj, acc):
        for u in range(unroll):
            acc = acc + weighted_values(pl.multiple_of((jj * unroll + u) * tk, tk), 0, False)
        return acc

    acc = lax.fori_loop(0, i * nsub // unroll, below_diagonal, jnp.zeros((tq, 2 * LANES), F32))
    for c in range(nsub):
        pv = weighted_values(pl.multiple_of(i * tq + c * tk, tk), c * tk, True)
        acc = acc + pv if c == 0 else jnp.concatenate([acc[:c * tk], acc[c * tk:] + pv], axis=0)
    z = z_ref[0].astype(F32)
    o_ref[0] = (acc[:, :LANES] / acc[:, LANES:LANES + 1] * _silu(z)).astype(o_ref.dtype)


def _fox_attn(y, ccol, crow, logit_bound):
    b, s, _ = y.shape
    h = FOX_HEADS
    crow3 = crow.reshape(b * h, 1, s)

    def specs(tq):
        return [
            pl.BlockSpec((1, tq, LANES), lambda bi, hi, i: (bi, i, hi)),
            pl.BlockSpec((1, s, LANES), lambda bi, hi, i: (bi, 0, h + hi)),
            pl.BlockSpec((1, s, LANES), lambda bi, hi, i: (bi, 0, 2 * h + hi)),
            pl.BlockSpec((1, tq, LANES), lambda bi, hi, i: (bi, i, 3 * h + hi)),
            pl.BlockSpec((1, tq, LANES), lambda bi, hi, i: (bi, i, 0)),
            pl.BlockSpec((1, 1, s), lambda bi, hi, i: (bi * h + hi, 0, 0)),
        ]

    def online():
        tq, bq, tk = min(FOX_TQ, s), FOX_BQ, FOX_TK
        return pl.pallas_call(
            functools.partial(_fox_attn_kernel, tq=tq, bq=bq, tk=tk),
            grid=(b, h, s // tq),
            in_specs=specs(tq),
            out_specs=pl.BlockSpec((1, tq, LANES), lambda bi, hi, i: (bi, i, hi)),
            out_shape=jax.ShapeDtypeStruct((b, s, FOX_WIDTH), BF16),
            compiler_params=_params("parallel", "parallel", "arbitrary"),
            name="fox_attn",
        )(y, y, y, y, ccol, crow3)

    def bounded():
        tq, tk = min(FOX_BOUNDED_TQ, s), FOX_TK
        unroll = tq // tk
        return pl.pallas_call(
            functools.partial(_fox_attn_bounded_kernel, tq=tq, tk=tk, unroll=unroll),
            grid=(b, h, s // tq),
            in_specs=specs(tq) + [pl.BlockSpec((1, LANES), lambda bi, hi, i: (0, 0))],
            out_specs=pl.BlockSpec((1, tq, LANES), lambda bi, hi, i: (bi, i, hi)),
            out_shape=jax.ShapeDtypeStruct((b, s, FOX_WIDTH), BF16),
            compiler_params=_params("parallel", "parallel", "arbitrary"),
            name="fox_attn_bounded",
        )(y, y, y, y, ccol, crow3, jnp.full((1, LANES), logit_bound, F32))

    return lax.cond(logit_bound <= FOX_MAX_FIXED_SHIFT, bounded, online)


def _fox_layer(x2d, prev, bsz, norm_w, w_in, b_f, q_gain, k_gain, tag):
    t, d = x2d.shape
    s = t // bsz
    w_main, w_ext = _split_w_in(w_in, 4 * FOX_WIDTH)
    scale = FOX_HEAD_DIM ** -0.5 * LOG2_E
    gain = jnp.concatenate([jnp.tile(q_gain * scale, FOX_HEADS), jnp.tile(k_gain, FOX_HEADS)]).reshape(1, -1)
    x2d, y, (ccol, crow) = _inproj(x2d, prev, norm_w, w_main, w_ext, gain, 2 * FOX_HEADS, BF16, "fox_front" + tag,
                                   forget_bias=b_f, seq_len=s)
    logit_bound = FOX_HEAD_DIM * jnp.max(jnp.abs(q_gain * scale)) * jnp.max(jnp.abs(k_gain)) * 1.02
    g = _fox_attn(y.reshape(bsz, s, -1), ccol.reshape(bsz, s, LANES), crow, logit_bound)
    return x2d, g.reshape(t, FOX_WIDTH)


def _gla_kernel(q_ref, k_ref, v_ref, z_ref, ext_ref, wup_ref, bg_ref, gain_ref, o_ref, state_ref, *, rows):
    chunk = GLA_CHUNK
    chunks = [slice(c * chunk, (c + 1) * chunk) for c in range(rows // chunk)]

    @pl.when(pl.program_id(2) == 0)
    def _():
        state_ref[...] = jnp.zeros_like(state_ref)

    gate = _dot(ext_ref[0].astype(BF16), wup_ref[...]) + bg_ref[...]
    log_a = _log_sigmoid(gate) * (1.0 / GLA_TAU)
    bcum = _chunk_cumsum(log_a, chunk)
    q = q_ref[0].astype(F32) * (GLA_KEY_DIM ** -0.5)
    k = k_ref[0].astype(F32)
    q_dec = (q * jnp.exp(bcum)).astype(BF16)
    k_inv = (k * jnp.exp(-bcum)).astype(BF16)
    row = lax.broadcasted_iota(jnp.int32, (chunk, chunk), 0)
    col = lax.broadcasted_iota(jnp.int32, (chunk, chunk), 1)
    causal = row >= col
    b_last = [bcum[sl][chunk - 1:chunk, :] for sl in chunks]
    k_dec = [(k[sl] * jnp.exp(b_last[c] - bcum[sl])).astype(BF16) for c, sl in enumerate(chunks)]
    v16 = [v_ref[0, sl, :] for sl in chunks]
    attn = [jnp.where(causal, _dot_nt(q_dec[sl], k_inv[sl]), 0.0).astype(BF16) for sl in chunks]
    update = [_dot(v16[c].astype(F32).T.astype(BF16), k_dec[c]) for c in range(len(chunks))]
    state_t = state_ref[...]
    s16 = []
    for c in range(len(chunks)):
        s16.append(state_t.astype(BF16))
        state_t = state_t * jnp.exp(b_last[c]) + update[c]
    state_ref[...] = state_t
    for c, sl in enumerate(chunks):
        o = _dot(attn[c], v16[c]) + _dot_nt(q_dec[sl], s16[c])
        ms = jnp.mean(o * o, axis=-1, keepdims=True)
        on = o * lax.rsqrt(ms + RMS_EPS) * gain_ref[...]
        o_ref[0, sl, :] = (on * _silu(z_ref[0, sl, :].astype(F32))).astype(o_ref.dtype)


def _gla_layer(x2d, prev, bsz, norm_w, w_in, w_gate_up, b_gate, o_gain):
    t, d = x2d.shape
    s = t // bsz
    h, dk, dv = GLA_HEADS, GLA_KEY_DIM, GLA_VAL_DIM
    n_main = 2 * h * dk + 2 * GLA_WIDTH
    w_main, w_ext = _split_w_in(w_in, n_main)
    x2d, y, ext = _inproj(x2d, prev, norm_w, w_main, w_ext, jnp.ones((1, LANES), F32), 0, BF16, "gla_front")
    y = y.reshape(bsz, s, n_main)
    ext = ext.reshape(bsz, s, LANES)
    wup = jnp.zeros((LANES, h * dk), F32).at[:GLA_RANK].set(w_gate_up).astype(BF16)
    rows = min(GLA_ROWS, s)
    kern = functools.partial(_gla_kernel, rows=rows)
    k_off = h * dk // dk
    v_off = 2 * h * dk // dv
    z_off = (2 * h * dk + GLA_WIDTH) // dv
    g = pl.pallas_call(
        kern,
        grid=(bsz, h, s // rows),
        in_specs=[
            pl.BlockSpec((1, rows, dk), lambda bi, hi, i: (bi, i, hi)),
            pl.BlockSpec((1, rows, dk), lambda bi, hi, i: (bi, i, k_off + hi)),
            pl.BlockSpec((1, rows, dv), lambda bi, hi, i: (bi, i, v_off + hi)),
            pl.BlockSpec((1, rows, dv), lambda bi, hi, i: (bi, i, z_off + hi)),
            pl.BlockSpec((1, rows, LANES), lambda bi, hi, i: (bi, i, 0)),
            pl.BlockSpec((LANES, dk), lambda bi, hi, i: (0, hi)),
            pl.BlockSpec((1, dk), lambda bi, hi, i: (0, hi)),
            pl.BlockSpec((1, dv), lambda bi, hi, i: (0, 0)),
        ],
        out_specs=pl.BlockSpec((1, rows, dv), lambda bi, hi, i: (bi, i, hi)),
        out_shape=jax.ShapeDtypeStruct((bsz, s, GLA_WIDTH), BF16),
        scratch_shapes=[pltpu.VMEM((dv, dk), F32)],
        compiler_params=_params("parallel", "parallel", "arbitrary"),
        name="gla_chunk",
    )(y, y, y, y, ext, wup, b_gate.reshape(1, -1), o_gain.reshape(1, -1))
    return x2d, g.reshape(t, GLA_WIDTH)


GDN_HISTORY = 8


def _gdn_conv_heads(u_ref, w_ref, hist_ref, rows):
    pad = GDN_HISTORY
    first = pl.program_id(1) == 0

    @pl.when(first)
    def _():
        hist_ref[0:pad, :] = jnp.zeros((pad, hist_ref.shape[1]), F32)

    @pl.when(jnp.logical_not(first))
    def _():
        hist_ref[0:pad, :] = hist_ref[rows:rows + pad, :]

    hist_ref[pad:pad + rows, :] = u_ref[0].astype(F32)
    heads = []
    for g in range(GDN_CONV_DIM // LANES):
        cols = slice(g * LANES, (g + 1) * LANES)
        acc = jnp.zeros((rows, LANES), F32)
        for tap in range(GDN_CONV):
            off = pad - (GDN_CONV - 1) + tap
            acc = acc + hist_ref[off:off + rows, cols] * w_ref[tap:tap + 1, cols]
        seg = _silu(acc)
        if g < 2 * GDN_QK_HEADS:
            ss = jnp.sum(seg * seg, axis=-1, keepdims=True)
            seg = seg * lax.rsqrt(ss + RMS_EPS)
            if g < GDN_QK_HEADS:
                seg = seg * (GDN_HEAD_DIM ** -0.5)
        heads.append(seg)
    n_hq = GDN_QK_HEADS
    return heads[:n_hq], heads[n_hq:2 * n_hq], heads[2 * n_hq:]


def _gdn_kernel(u_ref, z_ref, ext_ref, cw_ref, alog_ref, dtb_ref, gain_ref, o_ref, state_ref, hist_ref, *, rows):
    chunk = GDN_CHUNK
    d = GDN_HEAD_DIM
    n_hq, n_hv = GDN_QK_HEADS, GDN_V_HEADS
    rep = n_hv // n_hq
    chunks = [slice(c * chunk, (c + 1) * chunk) for c in range(rows // chunk)]
    pairs = [(hv, c) for hv in range(n_hv) for c in range(len(chunks))]

    @pl.when(pl.program_id(1) == 0)
    def _():
        state_ref[...] = jnp.zeros_like(state_ref)

    q, k, v_heads = _gdn_conv_heads(u_ref, cw_ref, hist_ref, rows)

    ext = ext_ref[0]
    g_all = -jnp.exp(alog_ref[...]) * _softplus(ext + dtb_ref[...])
    gc_all = _chunk_cumsum(g_all, chunk)
    gc_all_t = gc_all.T
    beta_all = _sigmoid(ext)
    row = lax.broadcasted_iota(jnp.int32, (chunk, chunk), 0)
    col = lax.broadcasted_iota(jnp.int32, (chunk, chunk), 1)
    causal = row >= col
    strict = row > col

    q16 = [x.astype(BF16) for x in q]
    k16 = [x.astype(BF16) for x in k]
    kk = {(h, c): _dot_nt(k16[h][sl], k16[h][sl]) for h in range(n_hq) for c, sl in enumerate(chunks)}
    qk = {(h, c): _dot_nt(q16[h][sl], k16[h][sl]) for h in range(n_hq) for c, sl in enumerate(chunks)}

    gc, rhs, q_dec = [], [], []
    for hv in range(n_hv):
        h = hv // rep
        gc_h = gc_all[:, hv:hv + 1]
        beta = beta_all[:, n_hv + hv:n_hv + hv + 1]
        eg = jnp.exp(gc_h)
        v = v_heads[hv]
        gc.append(gc_h)
        rhs.append(jnp.concatenate([v * beta, k[h] * (beta * eg)], axis=-1))
        q_dec.append((q[h] * eg).astype(BF16))

    decay, nk, m = {}, {}, {}
    for hv, c in pairs:
        sl = chunks[c]
        diff = gc[hv][sl] - gc_all_t[hv:hv + 1, sl]
        decay[hv, c] = jnp.where(causal, jnp.exp(jnp.where(causal, diff, 0.0)), 0.0)
        beta = beta_all[sl, n_hv + hv:n_hv + hv + 1]
        nk[hv, c] = -jnp.where(strict, kk[hv // rep, c] * beta * decay[hv, c], 0.0)
        m[hv, c] = nk[hv, c]
    span = 2
    while span < chunk:
        n16 = {p: nk[p].astype(BF16) for p in pairs}
        nk = {p: _dot(n16[p], n16[p]) for p in pairs}
        mn = {p: _dot(m[p].astype(BF16), nk[p].astype(BF16)) for p in pairs}
        m = {p: m[p] + nk[p] + mn[p] for p in pairs}
        span *= 2
    uw = {(hv, c): rhs[hv][chunks[c]] + _dot(m[hv, c].astype(BF16), rhs[hv][chunks[c]].astype(BF16))
          for hv, c in pairs}

    heads = range(n_hv)
    state = [state_ref[hv] for hv in heads]
    for c, sl in enumerate(chunks):
        g_last = [gc[hv][sl][chunk - 1:chunk, :] for hv in heads]
        k_dec_t = [(k[hv // rep][sl] * jnp.exp(g_last[hv] - gc[hv][sl])).T.astype(BF16) for hv in heads]
        qk16 = [(qk[hv // rep, c] * decay[hv, c]).astype(BF16) for hv in heads]
        s16 = [state[hv].astype(BF16) for hv in heads]
        ws = [_dot(uw[hv, c][:, d:].astype(BF16), s16[hv]) for hv in heads]
        qs = [_dot(q_dec[hv][sl], s16[hv]) for hv in heads]
        v16 = [(uw[hv, c][:, :d] - ws[hv]).astype(BF16) for hv in heads]
        o = [qs[hv] + _dot(qk16[hv], v16[hv]) for hv in heads]
        state = [state[hv] * jnp.exp(g_last[hv]) + _dot(k_dec_t[hv], v16[hv]) for hv in heads]
        for hv in heads:
            ms = jnp.mean(o[hv] * o[hv], axis=-1, keepdims=True)
            on = o[hv] * lax.rsqrt(ms + RMS_EPS) * gain_ref[...]
            z = z_ref[0, sl, hv * d:(hv + 1) * d].astype(F32)
            o_ref[0, sl, hv * d:(hv + 1) * d] = (on * _silu(z)).astype(o_ref.dtype)
    for hv in heads:
        state_ref[hv] = state[hv]


def _gdn_layer(x2d, prev, bsz, norm_w, w_in, conv_w, a_log, dt_bias, o_gain):
    t, dm = x2d.shape
    s = t // bsz
    hv, d = GDN_V_HEADS, GDN_HEAD_DIM
    n_main = GDN_CONV_DIM + GDN_WIDTH
    w_main, w_ext = _split_w_in(w_in, n_main)
    x2d, y, ext = _inproj(x2d, prev, norm_w, w_main, w_ext, jnp.ones((1, LANES), F32), 0, BF16, "gdn_front")
    y = y.reshape(bsz, s, n_main)
    ext = ext.reshape(bsz, s, LANES)
    alog = jnp.zeros((1, LANES), F32).at[0, :hv].set(a_log)
    dtb = jnp.zeros((1, LANES), F32).at[0, :hv].set(dt_bias)
    rows = min(GDN_ROWS, s)
    kern = functools.partial(_gdn_kernel, rows=rows)
    g = pl.pallas_call(
        kern,
        grid=(bsz, s // rows),
        in_specs=[
            pl.BlockSpec((1, rows, GDN_CONV_DIM), lambda bi, i: (bi, i, 0)),
            pl.BlockSpec((1, rows, GDN_WIDTH), lambda bi, i: (bi, i, GDN_CONV_DIM // GDN_WIDTH)),
            pl.BlockSpec((1, rows, LANES), lambda bi, i: (bi, i, 0)),
            pl.BlockSpec((GDN_CONV, GDN_CONV_DIM), lambda bi, i: (0, 0)),
            pl.BlockSpec((1, LANES), lambda bi, i: (0, 0)),
            pl.BlockSpec((1, LANES), lambda bi, i: (0, 0)),
            pl.BlockSpec((1, d), lambda bi, i: (0, 0)),
        ],
        out_specs=pl.BlockSpec((1, rows, GDN_WIDTH), lambda bi, i: (bi, i, 0)),
        out_shape=jax.ShapeDtypeStruct((bsz, s, GDN_WIDTH), BF16),
        scratch_shapes=[pltpu.VMEM((hv, d, d), F32), pltpu.VMEM((GDN_HISTORY + rows, GDN_CONV_DIM), F32)],
        compiler_params=_params("parallel", "arbitrary"),
        name="gdn_chunk",
    )(y, y, ext, conv_w, alog, dtb, o_gain.reshape(1, -1))
    return x2d, g.reshape(t, GDN_WIDTH)


def kernel(x, norm_w, fox_w_in, fox_b_f, fox_q_gain, fox_k_gain, fox_w_out, gla_w_in, gla_w_gate_up, gla_b_gate, gla_o_gain, gla_w_out, gdn_w_in, gdn_conv_w, gdn_a_log, gdn_dt_bias, gdn_o_gain, gdn_w_out):
    bsz, s, d = x.shape
    x2d = x.reshape(bsz * s, d)
    prev = None
    for layer in range(DEPTH):
        kind, idx = layer % N_MIXERS, layer // N_MIXERS
        if kind == 0:
            x2d, g = _fox_layer(x2d, prev, bsz, norm_w[layer], fox_w_in[idx], fox_b_f[idx], fox_q_gain[idx],
                                fox_k_gain[idx], str(idx))
            prev = (g, fox_w_out[idx])
        elif kind == 1:
            x2d, g = _gla_layer(x2d, prev, bsz, norm_w[layer], gla_w_in[idx], gla_w_gate_up[idx], gla_b_gate[idx],
                                gla_o_gain[idx])
            prev = (g, gla_w_out[idx])
        else:
            x2d, g = _gdn_layer(x2d, prev, bsz, norm_w[layer], gdn_w_in[idx], gdn_conv_w[idx], gdn_a_log[idx],
                                gdn_dt_bias[idx], gdn_o_gain[idx])
            prev = (g, gdn_w_out[idx])
    x2d = _outproj(prev[0], prev[1], x2d, "outproj_last")
    return x2d.reshape(bsz, s, d)
```

```python
import functools
import math

import jax
import jax.numpy as jnp
from jax import lax
from jax.experimental import pallas as pl
from jax.experimental.pallas import tpu as pltpu

F32 = jnp.float32
BF16 = jnp.bfloat16

D_MODEL = 1024
DEPTH = 4
N_MIXERS = 3
RMS_EPS = 1e-6
LANES = 128

FOX_HEADS = 8
FOX_HEAD_DIM = 128
FOX_WIDTH = 1024

GLA_HEADS = 4
GLA_KEY_DIM = 128
GLA_VAL_DIM = 256
GLA_WIDTH = 1024
GLA_RANK = 16
GLA_TAU = 16.0
GLA_CHUNK = 64

GDN_QK_HEADS = 4
GDN_V_HEADS = 8
GDN_HEAD_DIM = 128
GDN_WIDTH = 1024
GDN_CONV = 4
GDN_CHUNK = 64
GDN_CONV_DIM = 2048

VMEM_LIMIT_BYTES = 56 * 1024 * 1024

LOG2_E = math.log2(math.e)

FOX_TQ, FOX_BQ, FOX_TK = 1024, 512, 512
FOX_BOUNDED_TQ = 2048
FOX_MAX_FIXED_SHIFT = 40.0
GLA_ROWS = 512
GDN_ROWS = 256


def _params(*semantics):
    return pltpu.CompilerParams(dimension_semantics=semantics, vmem_limit_bytes=VMEM_LIMIT_BYTES)


def _dot(a, b):
    return jnp.dot(a, b, preferred_element_type=F32)


def _dot_nt(a, b):
    return lax.dot_general(a, b, (((1,), (1,)), ((), ())), preferred_element_type=F32)


def _sigmoid(x):
    return 1.0 / (1.0 + jnp.exp2(x * (-LOG2_E)))


def _silu(x):
    return x * _sigmoid(x)


def _softplus(x):
    return jnp.maximum(x, 0.0) + jnp.log(1.0 + jnp.exp(-jnp.abs(x)))


def _log_sigmoid(x):
    return -_softplus(-x)


def _chunk_cumsum(x, chunk):
    row = lax.broadcasted_iota(jnp.int32, x.shape, 0) & (chunk - 1)
    shift = 1
    while shift < chunk:
        x = x + jnp.where(row >= shift, pltpu.roll(x, shift, axis=0), 0.0)
        shift *= 2
    return x


def _inproj_kernel(*refs, tn, n_norm_groups, has_outproj, has_cumsum, tiles_per_seq):
    it = iter(refs)
    x_ref = next(it)
    g_ref, wo_ref = (next(it), next(it)) if has_outproj else (None, None)
    nw_ref, w_ref, we_ref, gain_ref = next(it), next(it), next(it), next(it)
    bias_ref = next(it) if has_cumsum else None
    xo_ref = next(it) if has_outproj else None
    o_ref = next(it)

    tm = x_ref.shape[0]
    xn = []
    for r in (slice(0, tm // 2), slice(tm // 2, tm)):
        x = x_ref[r, :]
        if has_outproj:
            x = x + _dot(g_ref[r, :], wo_ref[...])
            xo_ref[r, :] = x
        ms = jnp.mean(x * x, axis=-1, keepdims=True)
        xn.append((x * lax.rsqrt(ms + RMS_EPS) * nw_ref[...]).astype(BF16))
    xn = jnp.concatenate(xn, axis=0)
    extra = _dot(xn, we_ref[...])
    if has_cumsum:
        ccol_ref, crow_ref, carry_ref = next(it), next(it), next(it)

        @pl.when(pl.program_id(0) % tiles_per_seq == 0)
        def _():
            carry_ref[...] = jnp.zeros_like(carry_ref)

        log_f = _log_sigmoid(extra + bias_ref[...]) * LOG2_E
        tm = log_f.shape[0]
        cs = _chunk_cumsum(log_f, tm) + carry_ref[...]
        carry_ref[...] = cs[tm - 1:tm, :]
        ccol_ref[...] = cs
        crow_ref[0] = cs.T[:FOX_HEADS, :]
    else:
        next(it)[...] = extra
    n = o_ref.shape[1]
    for j in range(n // tn):
        acc = _dot(xn, w_ref[:, j * tn:(j + 1) * tn])
        for g in range(tn // LANES):
            col = j * tn + g * LANES
            seg = acc[:, g * LANES:(g + 1) * LANES]
            if col // LANES < n_norm_groups:
                msq = jnp.mean(seg * seg, axis=-1, keepdims=True)
                seg = seg * lax.rsqrt(msq + RMS_EPS) * gain_ref[:, col:col + LANES]
            o_ref[:, col:col + LANES] = seg.astype(o_ref.dtype)


def _inproj(x2d, prev, norm_w, w_main, w_ext, gain, n_norm_groups, out_dtype, name, forget_bias=None, seq_len=None):
    t, d = x2d.shape
    n = w_main.shape[1]
    tm, tn = 1024, 512
    has_outproj = prev is not None
    has_cumsum = forget_bias is not None
    row = lambda i: (i, 0)
    fixed = lambda i: (0, 0)
    resident = pl.Buffered(1)
    operands, in_specs = [x2d], [pl.BlockSpec((tm, d), row)]
    if has_outproj:
        g2d, w_out = prev
        operands += [g2d, w_out.astype(BF16)]
        in_specs += [pl.BlockSpec((tm, g2d.shape[1]), row), pl.BlockSpec(w_out.shape, fixed, pipeline_mode=resident)]
    operands += [norm_w.reshape(1, d), w_main, w_ext, gain]
    in_specs += [pl.BlockSpec((1, d), fixed), pl.BlockSpec((d, n), fixed, pipeline_mode=resident),
                 pl.BlockSpec((d, LANES), fixed),
                 pl.BlockSpec((1, gain.shape[1]), fixed)]
    out_shape, out_specs, scratch = [], [], []
    if has_outproj:
        out_shape.append(jax.ShapeDtypeStruct((t, d), F32))
        out_specs.append(pl.BlockSpec((tm, d), row))
    out_shape.append(jax.ShapeDtypeStruct((t, n), out_dtype))
    out_specs.append(pl.BlockSpec((tm, n), row))
    out_shape.append(jax.ShapeDtypeStruct((t, LANES), F32))
    out_specs.append(pl.BlockSpec((tm, LANES), row))
    tiles_per_seq = None
    if has_cumsum:
        tiles_per_seq = seq_len // tm
        operands += [jnp.zeros((1, LANES), F32).at[0, :FOX_HEADS].set(forget_bias)]
        in_specs += [pl.BlockSpec((1, LANES), fixed)]
        out_shape.append(jax.ShapeDtypeStruct((t // seq_len, FOX_HEADS, seq_len), F32))
        out_specs.append(pl.BlockSpec((1, FOX_HEADS, tm), lambda i: (i // tiles_per_seq, 0, i % tiles_per_seq)))
        scratch.append(pltpu.VMEM((1, LANES), F32))
    kern = functools.partial(_inproj_kernel, tn=tn, n_norm_groups=n_norm_groups, has_outproj=has_outproj,
                             has_cumsum=has_cumsum, tiles_per_seq=tiles_per_seq)
    outs = pl.pallas_call(
        kern,
        grid=(t // tm,),
        in_specs=in_specs,
        out_specs=out_specs,
        out_shape=out_shape,
        scratch_shapes=scratch,
        compiler_params=_params("arbitrary" if has_cumsum else "parallel"),
        name=name,
    )(*operands)
    x_out = outs[0] if has_outproj else x2d
    rest = outs[1:] if has_outproj else outs
    return x_out, rest[0], (tuple(rest[1:]) if has_cumsum else rest[1])


def _split_w_in(w_in, n_main):
    d, n = w_in.shape
    w_main = w_in[:, :n_main].astype(BF16)
    w_ext = jnp.zeros((d, LANES), F32).at[:, :n - n_main].set(w_in[:, n_main:]).astype(BF16)
    return w_main, w_ext


def _outproj_kernel(g_ref, w_ref, x_ref, o_ref):
    o_ref[...] = x_ref[...] + _dot(g_ref[...], w_ref[...])


def _outproj(g2d, w_out, x2d, name):
    t, d = x2d.shape
    k = g2d.shape[1]
    tm = 512
    return pl.pallas_call(
        _outproj_kernel,
        grid=(t // tm,),
        in_specs=[
            pl.BlockSpec((tm, k), lambda i: (i, 0)),
            pl.BlockSpec((k, d), lambda i: (0, 0)),
            pl.BlockSpec((tm, d), lambda i: (i, 0)),
        ],
        out_specs=pl.BlockSpec((tm, d), lambda i: (i, 0)),
        out_shape=jax.ShapeDtypeStruct((t, d), F32),
        compiler_params=_params("parallel"),
        name=name,
    )(g2d, w_out.astype(BF16), x2d)


def _fox_attn_kernel(q_ref, k_ref, v_ref, z_ref, ccol_ref, crow_ref, o_ref, *, tq, bq, tk):
    h = pl.program_id(1)
    i = pl.program_id(2)
    nsub = tq // bq
    lane = lax.broadcasted_iota(jnp.int32, (bq, LANES), 1)
    qs = [q_ref[0, r * bq:(r + 1) * bq, :] for r in range(nsub)]
    cqs = [jnp.sum(jnp.where(lane == h, ccol_ref[0, r * bq:(r + 1) * bq, :], 0.0), axis=-1, keepdims=True)
           for r in range(nsub)]
    row = lax.broadcasted_iota(jnp.int32, (bq, tk), 0)
    col = lax.broadcasted_iota(jnp.int32, (bq, tk), 1)

    def load_keys(start):
        return (k_ref[0, pl.ds(start, tk), :], v_ref[0, pl.ds(start, tk), :], crow_ref[0, pl.ds(h, 1), pl.ds(start, tk)])

    def attend(r, carry, keys, row_minus_col):
        m, l, acc = carry
        k, v, ck = keys
        t = _dot_nt(qs[r], k) - ck
        if row_minus_col is not None:
            t = jnp.where(row + row_minus_col >= col, t, -jnp.inf)
        m_new = jnp.maximum(m, jnp.max(t, axis=-1, keepdims=True) + cqs[r])
        alpha = jnp.exp2(m - m_new)
        p = jnp.exp2(t + (cqs[r] - m_new))
        l = alpha * l + jnp.sum(p, axis=-1, keepdims=True)
        acc = alpha * acc + _dot(p.astype(BF16), v)
        return m_new, l, acc

    def below_diagonal(kb, carries):
        keys = load_keys(pl.multiple_of(kb * tk, tk))
        return tuple(attend(r, carries[r], keys, None) for r in range(nsub))

    init = tuple((jnp.full((bq, 1), -jnp.inf, F32), jnp.zeros((bq, 1), F32), jnp.zeros((bq, LANES), F32))
                 for _ in range(nsub))
    carries = list(lax.fori_loop(0, i * (tq // tk), below_diagonal, init))
    for c in range(tq // tk):
        keys = load_keys(pl.multiple_of(i * tq + c * tk, tk))
        for r in range(nsub):
            row0, col0 = r * bq, c * tk
            if col0 > row0 + bq - 1:
                continue
            crosses = col0 + tk - 1 > row0
            carries[r] = attend(r, carries[r], keys, (row0 - col0) if crosses else None)
    for r in range(nsub):
        _, l, acc = carries[r]
        z = z_ref[0, r * bq:(r + 1) * bq, :].astype(F32)
        o_ref[0, r * bq:(r + 1) * bq, :] = (acc / l * _silu(z)).astype(o_ref.dtype)


def _fox_attn_bounded_kernel(q_ref, k_ref, v_ref, z_ref, ccol_ref, crow_ref, bound_ref, o_ref, *, tq, tk, unroll):
    h = pl.program_id(1)
    i = pl.program_id(2)
    nsub = tq // tk
    lane = lax.broadcasted_iota(jnp.int32, (tq, LANES), 1)
    cq = jnp.sum(jnp.where(lane == h, ccol_ref[0], 0.0), axis=-1, keepdims=True)
    dq = cq - bound_ref[0:1, 0:1]
    q = q_ref[0]
    ones = jnp.ones((tk, LANES), BF16)

    def weighted_values(start, first_row, diagonal):
        k = k_ref[0, pl.ds(start, tk), :]
        v1 = jnp.concatenate([v_ref[0, pl.ds(start, tk), :], ones], axis=1)
        ck = crow_ref[0, pl.ds(h, 1), pl.ds(start, tk)]
        e = (_dot_nt(q[first_row:], k) - ck) + dq[first_row:]
        if diagonal:
            row = lax.broadcasted_iota(jnp.int32, (tk, tk), 0)
            col = lax.broadcasted_iota(jnp.int32, (tk, tk), 1)
            top = jnp.where(row >= col, e[:tk], -jnp.inf)
            e = top if e.shape[0] == tk else jnp.concatenate([top, e[tk:]], axis=0)
        return _dot(jnp.exp2(e).astype(BF16), v1)

    def below_diagonal(jj, acc):
        for u in range(unroll):
            acc = acc + weighted_values(pl.multiple_of((jj * unroll + u) * tk, tk), 0, False)
        return acc

    acc = lax.fori_loop(0, i * nsub // unroll, below_diagonal, jnp.zeros((tq, 2 * LANES), F32))
    for c in range(nsub):
        pv = weighted_values(pl.multiple_of(i * tq + c * tk, tk), c * tk, True)
        acc = acc + pv if c == 0 else jnp.concatenate([acc[:c * tk], acc[c * tk:] + pv], axis=0)
    z = z_ref[0].astype(F32)
    o_ref[0] = (acc[:, :LANES] / acc[:, LANES:LANES + 1] * _silu(z)).astype(o_ref.dtype)


def _fox_attn(y, ccol, crow, logit_bound):
    b, s, _ = y.shape
    h = FOX_HEADS

    def specs(tq):
        return [
            pl.BlockSpec((1, tq, LANES), lambda bi, hi, i: (bi, i, hi)),
            pl.BlockSpec((1, s, LANES), lambda bi, hi, i: (bi, 0, h + hi)),
            pl.BlockSpec((1, s, LANES), lambda bi, hi, i: (bi, 0, 2 * h + hi)),
            pl.BlockSpec((1, tq, LANES), lambda bi, hi, i: (bi, i, 3 * h + hi)),
            pl.BlockSpec((1, tq, LANES), lambda bi, hi, i: (bi, i, 0)),
            pl.BlockSpec((1, h, s), lambda bi, hi, i: (bi, 0, 0)),
        ]

    def online():
        tq, bq, tk = min(FOX_TQ, s), FOX_BQ, FOX_TK
        return pl.pallas_call(
            functools.partial(_fox_attn_kernel, tq=tq, bq=bq, tk=tk),
            grid=(b, h, s // tq),
            in_specs=specs(tq),
            out_specs=pl.BlockSpec((1, tq, LANES), lambda bi, hi, i: (bi, i, hi)),
            out_shape=jax.ShapeDtypeStruct((b, s, FOX_WIDTH), BF16),
            compiler_params=_params("parallel", "parallel", "arbitrary"),
            name="fox_attn",
        )(y, y, y, y, ccol, crow)

    def bounded():
        tq, tk = min(FOX_BOUNDED_TQ, s), FOX_TK
        unroll = tq // tk
        return pl.pallas_call(
            functools.partial(_fox_attn_bounded_kernel, tq=tq, tk=tk, unroll=unroll),
            grid=(b, h, s // tq),
            in_specs=specs(tq) + [pl.BlockSpec((1, LANES), lambda bi, hi, i: (0, 0))],
            out_specs=pl.BlockSpec((1, tq, LANES), lambda bi, hi, i: (bi, i, hi)),
            out_shape=jax.ShapeDtypeStruct((b, s, FOX_WIDTH), BF16),
            compiler_params=_params("parallel", "parallel", "arbitrary"),
            name="fox_attn_bounded",
        )(y, y, y, y, ccol, crow, jnp.full((1, LANES), logit_bound, F32))

    return lax.cond(logit_bound <= FOX_MAX_FIXED_SHIFT, bounded, online)


def _fox_layer(x2d, prev, bsz, norm_w, w_in, b_f, q_gain, k_gain, tag):
    t, d = x2d.shape
    s = t // bsz
    w_main, w_ext = _split_w_in(w_in, 4 * FOX_WIDTH)
    scale = FOX_HEAD_DIM ** -0.5 * LOG2_E
    gain = jnp.concatenate([jnp.tile(q_gain * scale, FOX_HEADS), jnp.tile(k_gain, FOX_HEADS)]).reshape(1, -1)
    x2d, y, (ccol, crow) = _inproj(x2d, prev, norm_w, w_main, w_ext, gain, 2 * FOX_HEADS, BF16, "fox_front" + tag,
                                   forget_bias=b_f, seq_len=s)
    logit_bound = FOX_HEAD_DIM * jnp.max(jnp.abs(q_gain * scale)) * jnp.max(jnp.abs(k_gain)) * 1.02
    g = _fox_attn(y.reshape(bsz, s, -1), ccol.reshape(bsz, s, LANES), crow, logit_bound)
    return x2d, g.reshape(t, FOX_WIDTH)


def _gla_kernel(q_ref, k_ref, v_ref, z_ref, ext_ref, wup_ref, bg_ref, gain_ref, o_ref, state_ref, *, rows):
    chunk = GLA_CHUNK
    chunks = [slice(c * chunk, (c + 1) * chunk) for c in range(rows // chunk)]

    @pl.when(pl.program_id(2) == 0)
    def _():
        state_ref[...] = jnp.zeros_like(state_ref)

    gate = _dot(ext_ref[0].astype(BF16), wup_ref[...]) + bg_ref[...]
    log_a = _log_sigmoid(gate) * (1.0 / GLA_TAU)
    bcum = _chunk_cumsum(log_a, chunk)
    q = q_ref[0].astype(F32) * (GLA_KEY_DIM ** -0.5)
    k = k_ref[0].astype(F32)
    q_dec = (q * jnp.exp(bcum)).astype(BF16)
    k_inv = (k * jnp.exp(-bcum)).astype(BF16)
    row = lax.broadcasted_iota(jnp.int32, (chunk, chunk), 0)
    col = lax.broadcasted_iota(jnp.int32, (chunk, chunk), 1)
    causal = row >= col
    b_last = [bcum[sl][chunk - 1:chunk, :] for sl in chunks]
    k_dec = [(k[sl] * jnp.exp(b_last[c] - bcum[sl])).astype(BF16) for c, sl in enumerate(chunks)]
    v16 = [v_ref[0, sl, :] for sl in chunks]
    attn = [jnp.where(causal, _dot_nt(q_dec[sl], k_inv[sl]), 0.0).astype(BF16) for sl in chunks]
    update = [_dot(v16[c].astype(F32).T.astype(BF16), k_dec[c]) for c in range(len(chunks))]
    state_t = state_ref[...]
    s16 = []
    for c in range(len(chunks)):
        s16.append(state_t.astype(BF16))
        state_t = state_t * jnp.exp(b_last[c]) + update[c]
    state_ref[...] = state_t
    for c, sl in enumerate(chunks):
        o = _dot(attn[c], v16[c]) + _dot_nt(q_dec[sl], s16[c])
        ms = jnp.mean(o * o, axis=-1, keepdims=True)
        on = o * lax.rsqrt(ms + RMS_EPS) * gain_ref[...]
        o_ref[0, sl, :] = (on * _silu(z_ref[0, sl, :].astype(F32))).astype(o_ref.dtype)


def _gla_layer(x2d, prev, bsz, norm_w, w_in, w_gate_up, b_gate, o_gain):
    t, d = x2d.shape
    s = t // bsz
    h, dk, dv = GLA_HEADS, GLA_KEY_DIM, GLA_VAL_DIM
    n_main = 2 * h * dk + 2 * GLA_WIDTH
    w_main, w_ext = _split_w_in(w_in, n_main)
    x2d, y, ext = _inproj(x2d, prev, norm_w, w_main, w_ext, jnp.ones((1, LANES), F32), 0, BF16, "gla_front")
    y = y.reshape(bsz, s, n_main)
    ext = ext.reshape(bsz, s, LANES)
    wup = jnp.zeros((LANES, h * dk), F32).at[:GLA_RANK].set(w_gate_up).astype(BF16)
    rows = min(GLA_ROWS, s)
    kern = functools.partial(_gla_kernel, rows=rows)
    k_off = h * dk // dk
    v_off = 2 * h * dk // dv
    z_off = (2 * h * dk + GLA_WIDTH) // dv
    g = pl.pallas_call(
        kern,
        grid=(bsz, h, s // rows),
        in_specs=[
            pl.BlockSpec((1, rows, dk), lambda bi, hi, i: (bi, i, hi)),
            pl.BlockSpec((1, rows, dk), lambda bi, hi, i: (bi, i, k_off + hi)),
            pl.BlockSpec((1, rows, dv), lambda bi, hi, i: (bi, i, v_off + hi)),
            pl.BlockSpec((1, rows, dv), lambda bi, hi, i: (bi, i, z_off + hi)),
            pl.BlockSpec((1, rows, LANES), lambda bi, hi, i: (bi, i, 0)),
            pl.BlockSpec((LANES, dk), lambda bi, hi, i: (0, hi)),
            pl.BlockSpec((1, dk), lambda bi, hi, i: (0, hi)),
            pl.BlockSpec((1, dv), lambda bi, hi, i: (0, 0)),
        ],
        out_specs=pl.BlockSpec((1, rows, dv), lambda bi, hi, i: (bi, i, hi)),
        out_shape=jax.ShapeDtypeStruct((bsz, s, GLA_WIDTH), BF16),
        scratch_shapes=[pltpu.VMEM((dv, dk), F32)],
        compiler_params=_params("parallel", "parallel", "arbitrary"),
        name="gla_chunk",
    )(y, y, y, y, ext, wup, b_gate.reshape(1, -1), o_gain.reshape(1, -1))
    return x2d, g.reshape(t, GLA_WIDTH)


GDN_HISTORY = 8


def _gdn_conv_heads(u_ref, w_ref, hist_ref, rows):
    pad = GDN_HISTORY
    first = pl.program_id(1) == 0

    @pl.when(first)
    def _():
        hist_ref[0:pad, :] = jnp.zeros((pad, hist_ref.shape[1]), F32)

    @pl.when(jnp.logical_not(first))
    def _():
        hist_ref[0:pad, :] = hist_ref[rows:rows + pad, :]

    hist_ref[pad:pad + rows, :] = u_ref[0].astype(F32)
    heads = []
    for g in range(GDN_CONV_DIM // LANES):
        cols = slice(g * LANES, (g + 1) * LANES)
        acc = hist_ref[pad:pad + rows, cols] * w_ref[GDN_CONV - 1:GDN_CONV, cols]
        for tap in range(GDN_CONV - 1):
            off = pad - (GDN_CONV - 1) + tap
            acc = acc + hist_ref[off:off + rows, cols] * w_ref[tap:tap + 1, cols]
        seg = _silu(acc)
        if g < 2 * GDN_QK_HEADS:
            ss = jnp.sum(seg * seg, axis=-1, keepdims=True)
            seg = seg * lax.rsqrt(ss + RMS_EPS)
            if g < GDN_QK_HEADS:
                seg = seg * (GDN_HEAD_DIM ** -0.5)
        heads.append(seg)
    n_hq = GDN_QK_HEADS
    return heads[:n_hq], heads[n_hq:2 * n_hq], heads[2 * n_hq:]


def _gdn_kernel(u_ref, z_ref, ext_ref, cw_ref, alog_ref, dtb_ref, gain_ref, o_ref, state_ref, hist_ref, *, rows):
    chunk = GDN_CHUNK
    d = GDN_HEAD_DIM
    n_hq, n_hv = GDN_QK_HEADS, GDN_V_HEADS
    rep = n_hv // n_hq
    chunks = [slice(c * chunk, (c + 1) * chunk) for c in range(rows // chunk)]
    pairs = [(hv, c) for hv in range(n_hv) for c in range(len(chunks))]

    @pl.when(pl.program_id(1) == 0)
    def _():
        state_ref[...] = jnp.zeros_like(state_ref)

    q, k, v_heads = _gdn_conv_heads(u_ref, cw_ref, hist_ref, rows)

    ext = ext_ref[0]
    g_all = -jnp.exp(alog_ref[...]) * _softplus(ext + dtb_ref[...])
    gc_all = _chunk_cumsum(g_all, chunk)
    gc_all_t = gc_all.T
    beta_all = _sigmoid(ext)
    row = lax.broadcasted_iota(jnp.int32, (chunk, chunk), 0)
    col = lax.broadcasted_iota(jnp.int32, (chunk, chunk), 1)
    causal = row >= col
    strict = row > col

    q16 = [x.astype(BF16) for x in q]
    k16 = [x.astype(BF16) for x in k]
    kk = {(h, c): _dot_nt(k16[h][sl], k16[h][sl]) for h in range(n_hq) for c, sl in enumerate(chunks)}
    qk = {(h, c): _dot_nt(q16[h][sl], k16[h][sl]) for h in range(n_hq) for c, sl in enumerate(chunks)}

    gc, rhs, q_dec = [], [], []
    for hv in range(n_hv):
        h = hv // rep
        gc_h = gc_all[:, hv:hv + 1]
        beta = beta_all[:, n_hv + hv:n_hv + hv + 1]
        eg = jnp.exp(gc_h)
        v = v_heads[hv]
        gc.append(gc_h)
        rhs.append(jnp.concatenate([v * beta, k[h] * (beta * eg)], axis=-1))
        q_dec.append((q[h] * eg).astype(BF16))

    decay, nk, m = {}, {}, {}
    for hv, c in pairs:
        sl = chunks[c]
        diff = gc[hv][sl] - gc_all_t[hv:hv + 1, sl]
        decay[hv, c] = jnp.where(causal, jnp.exp(jnp.where(causal, diff, 0.0)), 0.0)
        beta = beta_all[sl, n_hv + hv:n_hv + hv + 1]
        nk[hv, c] = -jnp.where(strict, kk[hv // rep, c] * beta * decay[hv, c], 0.0)
        m[hv, c] = nk[hv, c]
    span = 2
    while span < chunk:
        n16 = {p: nk[p].astype(BF16) for p in pairs}
        nk = {p: _dot(n16[p], n16[p]) for p in pairs}
        mn = {p: _dot(m[p].astype(BF16), nk[p].astype(BF16)) for p in pairs}
        m = {p: m[p] + nk[p] + mn[p] for p in pairs}
        span *= 2
    uw = {(hv, c): rhs[hv][chunks[c]] + _dot(m[hv, c].astype(BF16), rhs[hv][chunks[c]].astype(BF16))
          for hv, c in pairs}

    heads = range(n_hv)
    state = [state_ref[hv] for hv in heads]
    for c, sl in enumerate(chunks):
        g_last = [gc[hv][sl][chunk - 1:chunk, :] for hv in heads]
        k_dec_t = [(k[hv // rep][sl] * jnp.exp(g_last[hv] - gc[hv][sl])).T.astype(BF16) for hv in heads]
        qk16 = [(qk[hv // rep, c] * decay[hv, c]).astype(BF16) for hv in heads]
        s16 = [state[hv].astype(BF16) for hv in heads]
        ws = [_dot(uw[hv, c][:, d:].astype(BF16), s16[hv]) for hv in heads]
        qs = [_dot(q_dec[hv][sl], s16[hv]) for hv in heads]
        v16 = [(uw[hv, c][:, :d] - ws[hv]).astype(BF16) for hv in heads]
        o = [qs[hv] + _dot(qk16[hv], v16[hv]) for hv in heads]
        state = [state[hv] * jnp.exp(g_last[hv]) + _dot(k_dec_t[hv], v16[hv]) for hv in heads]
        for hv in heads:
            ms = jnp.mean(o[hv] * o[hv], axis=-1, keepdims=True)
            on = o[hv] * lax.rsqrt(ms + RMS_EPS) * gain_ref[...]
            z = z_ref[0, sl, hv * d:(hv + 1) * d].astype(F32)
            o_ref[0, sl, hv * d:(hv + 1) * d] = (on * _silu(z)).astype(o_ref.dtype)
    for hv in heads:
        state_ref[hv] = state[hv]


def _gdn_layer(x2d, prev, bsz, norm_w, w_in, conv_w, a_log, dt_bias, o_gain):
    t, dm = x2d.shape
    s = t // bsz
    hv, d = GDN_V_HEADS, GDN_HEAD_DIM
    n_main = GDN_CONV_DIM + GDN_WIDTH
    w_main, w_ext = _split_w_in(w_in, n_main)
    x2d, y, ext = _inproj(x2d, prev, norm_w, w_main, w_ext, jnp.ones((1, LANES), F32), 0, BF16, "gdn_front")
    y = y.reshape(bsz, s, n_main)
    ext = ext.reshape(bsz, s, LANES)
    alog = jnp.zeros((1, LANES), F32).at[0, :hv].set(a_log)
    dtb = jnp.zeros((1, LANES), F32).at[0, :hv].set(dt_bias)
    rows = min(GDN_ROWS, s)
    kern = functools.partial(_gdn_kernel, rows=rows)
    g = pl.pallas_call(
        kern,
        grid=(bsz, s // rows),
        in_specs=[
            pl.BlockSpec((1, rows, GDN_CONV_DIM), lambda bi, i: (bi, i, 0)),
            pl.BlockSpec((1, rows, GDN_WIDTH), lambda bi, i: (bi, i, GDN_CONV_DIM // GDN_WIDTH)),
            pl.BlockSpec((1, rows, LANES), lambda bi, i: (bi, i, 0)),
            pl.BlockSpec((GDN_CONV, GDN_CONV_DIM), lambda bi, i: (0, 0)),
            pl.BlockSpec((1, LANES), lambda bi, i: (0, 0)),
            pl.BlockSpec((1, LANES), lambda bi, i: (0, 0)),
            pl.BlockSpec((1, d), lambda bi, i: (0, 0)),
        ],
        out_specs=pl.BlockSpec((1, rows, GDN_WIDTH), lambda bi, i: (bi, i, 0)),
        out_shape=jax.ShapeDtypeStruct((bsz, s, GDN_WIDTH), BF16),
        scratch_shapes=[pltpu.VMEM((hv, d, d), F32), pltpu.VMEM((GDN_HISTORY + rows, GDN_CONV_DIM), F32)],
        compiler_params=_params("parallel", "arbitrary"),
        name="gdn_chunk",
    )(y, y, ext, conv_w, alog, dtb, o_gain.reshape(1, -1))
    return x2d, g.reshape(t, GDN_WIDTH)


def kernel(x, norm_w, fox_w_in, fox_b_f, fox_q_gain, fox_k_gain, fox_w_out, gla_w_in, gla_w_gate_up, gla_b_gate, gla_o_gain, gla_w_out, gdn_w_in, gdn_conv_w, gdn_a_log, gdn_dt_bias, gdn_o_gain, gdn_w_out):
    bsz, s, d = x.shape
    x2d = x.reshape(bsz * s, d)
    prev = None
    for layer in range(DEPTH):
        kind, idx = layer % N_MIXERS, layer // N_MIXERS
        if kind == 0:
            x2d, g = _fox_layer(x2d, prev, bsz, norm_w[layer], fox_w_in[idx], fox_b_f[idx], fox_q_gain[idx],
                                fox_k_gain[idx], str(idx))
            prev = (g, fox_w_out[idx])
        elif kind == 1:
            x2d, g = _gla_layer(x2d, prev, bsz, norm_w[layer], gla_w_in[idx], gla_w_gate_up[idx], gla_b_gate[idx],
                                gla_o_gain[idx])
            prev = (g, gla_w_out[idx])
        else:
            x2d, g = _gdn_layer(x2d, prev, bsz, norm_w[layer], gdn_w_in[idx], gdn_conv_w[idx], gdn_a_log[idx],
                                gdn_dt_bias[idx], gdn_o_gain[idx])
            prev = (g, gdn_w_out[idx])
    x2d = _outproj(prev[0], prev[1], x2d, "outproj_last")
    return x2d.reshape(bsz, s, d)
```

```python
import functools
import math

import jax
import jax.numpy as jnp
from jax import lax
from jax.experimental import pallas as pl
from jax.experimental.pallas import tpu as pltpu

F32 = jnp.float32
BF16 = jnp.bfloat16

D_MODEL = 1024
DEPTH = 4
N_MIXERS = 3
RMS_EPS = 1e-6
LANES = 128

FOX_HEADS = 8
FOX_HEAD_DIM = 128
FOX_WIDTH = 1024

GLA_HEADS = 4
GLA_KEY_DIM = 128
GLA_VAL_DIM = 256
GLA_WIDTH = 1024
GLA_RANK = 16
GLA_TAU = 16.0
GLA_CHUNK = 64

GDN_QK_HEADS = 4
GDN_V_HEADS = 8
GDN_HEAD_DIM = 128
GDN_WIDTH = 1024
GDN_CONV = 4
GDN_CHUNK = 64
GDN_CONV_DIM = 2048

VMEM_LIMIT_BYTES = 56 * 1024 * 1024

LOG2_E = math.log2(math.e)

FOX_TQ, FOX_BQ, FOX_TK = 1024, 512, 512
FOX_BOUNDED_TQ = 2048
FOX_MAX_FIXED_SHIFT = 40.0
GLA_ROWS = 2048
GDN_ROWS = 256


def _params(*semantics):
    return pltpu.CompilerParams(dimension_semantics=semantics, vmem_limit_bytes=VMEM_LIMIT_BYTES)


def _dot(a, b):
    return jnp.dot(a, b, preferred_element_type=F32)


def _dot_nt(a, b):
    return lax.dot_general(a, b, (((1,), (1,)), ((), ())), preferred_element_type=F32)


def _sigmoid(x):
    return 1.0 / (1.0 + jnp.exp2(x * (-LOG2_E)))


def _silu(x):
    return x * _sigmoid(x)


def _softplus(x):
    return jnp.maximum(x, 0.0) + jnp.log(1.0 + jnp.exp(-jnp.abs(x)))


def _log_sigmoid(x):
    return -_softplus(-x)


def _chunk_cumsum(x, chunk):
    row = lax.broadcasted_iota(jnp.int32, x.shape, 0) & (chunk - 1)
    shift = 1
    while shift < chunk:
        x = x + jnp.where(row >= shift, pltpu.roll(x, shift, axis=0), 0.0)
        shift *= 2
    return x


def _inproj_kernel(*refs, tn, n_norm_groups, has_outproj, has_cumsum, tiles_per_seq):
    it = iter(refs)
    x_ref = next(it)
    g_ref, wo_ref = (next(it), next(it)) if has_outproj else (None, None)
    nw_ref, w_ref, we_ref, gain_ref = next(it), next(it), next(it), next(it)
    bias_ref = next(it) if has_cumsum else None
    xo_ref = next(it) if has_outproj else None
    o_ref = next(it)

    tm = x_ref.shape[0]
    xn = []
    for r in (slice(0, tm // 2), slice(tm // 2, tm)):
        x = x_ref[r, :]
        if has_outproj:
            x = x + _dot(g_ref[r, :], wo_ref[...])
            xo_ref[r, :] = x
        ms = jnp.mean(x * x, axis=-1, keepdims=True)
        xn.append((x * lax.rsqrt(ms + RMS_EPS) * nw_ref[...]).astype(BF16))
    xn = jnp.concatenate(xn, axis=0)
    extra = _dot(xn, we_ref[...])
    if has_cumsum:
        ccol_ref, crow_ref, carry_ref = next(it), next(it), next(it)

        @pl.when(pl.program_id(0) % tiles_per_seq == 0)
        def _():
            carry_ref[...] = jnp.zeros_like(carry_ref)

        log_f = _log_sigmoid(extra + bias_ref[...]) * LOG2_E
        tm = log_f.shape[0]
        cs = _chunk_cumsum(log_f, tm) + carry_ref[...]
        carry_ref[...] = cs[tm - 1:tm, :]
        ccol_ref[...] = cs
        crow_ref[0] = cs.T[:FOX_HEADS, :]
    else:
        next(it)[...] = extra
    n = o_ref.shape[1]
    for j in range(n // tn):
        acc = _dot(xn, w_ref[:, j * tn:(j + 1) * tn])
        for g in range(tn // LANES):
            col = j * tn + g * LANES
            seg = acc[:, g * LANES:(g + 1) * LANES]
            if col // LANES < n_norm_groups:
                msq = jnp.mean(seg * seg, axis=-1, keepdims=True)
                seg = seg * lax.rsqrt(msq + RMS_EPS) * gain_ref[:, col:col + LANES]
            o_ref[:, col:col + LANES] = seg.astype(o_ref.dtype)


def _inproj(x2d, prev, norm_w, w_main, w_ext, gain, n_norm_groups, out_dtype, name, forget_bias=None, seq_len=None):
    t, d = x2d.shape
    n = w_main.shape[1]
    tm, tn = 1024, 512
    has_outproj = prev is not None
    has_cumsum = forget_bias is not None
    row = lambda i: (i, 0)
    fixed = lambda i: (0, 0)
    resident = pl.Buffered(1)
    operands, in_specs = [x2d], [pl.BlockSpec((tm, d), row)]
    if has_outproj:
        g2d, w_out = prev
        operands += [g2d, w_out.astype(BF16)]
        in_specs += [pl.BlockSpec((tm, g2d.shape[1]), row), pl.BlockSpec(w_out.shape, fixed, pipeline_mode=resident)]
    operands += [norm_w.reshape(1, d), w_main, w_ext, gain]
    in_specs += [pl.BlockSpec((1, d), fixed), pl.BlockSpec((d, n), fixed, pipeline_mode=resident),
                 pl.BlockSpec((d, LANES), fixed),
                 pl.BlockSpec((1, gain.shape[1]), fixed)]
    out_shape, out_specs, scratch = [], [], []
    if has_outproj:
        out_shape.append(jax.ShapeDtypeStruct((t, d), F32))
        out_specs.append(pl.BlockSpec((tm, d), row))
    out_shape.append(jax.ShapeDtypeStruct((t, n), out_dtype))
    out_specs.append(pl.BlockSpec((tm, n), row))
    out_shape.append(jax.ShapeDtypeStruct((t, LANES), F32))
    out_specs.append(pl.BlockSpec((tm, LANES), row))
    tiles_per_seq = None
    if has_cumsum:
        tiles_per_seq = seq_len // tm
        operands += [jnp.zeros((1, LANES), F32).at[0, :FOX_HEADS].set(forget_bias)]
        in_specs += [pl.BlockSpec((1, LANES), fixed)]
        out_shape.append(jax.ShapeDtypeStruct((t // seq_len, FOX_HEADS, seq_len), F32))
        out_specs.append(pl.BlockSpec((1, FOX_HEADS, tm), lambda i: (i // tiles_per_seq, 0, i % tiles_per_seq)))
        scratch.append(pltpu.VMEM((1, LANES), F32))
    kern = functools.partial(_inproj_kernel, tn=tn, n_norm_groups=n_norm_groups, has_outproj=has_outproj,
                             has_cumsum=has_cumsum, tiles_per_seq=tiles_per_seq)
    outs = pl.pallas_call(
        kern,
        grid=(t // tm,),
        in_specs=in_specs,
        out_specs=out_specs,
        out_shape=out_shape,
        scratch_shapes=scratch,
        compiler_params=_params("arbitrary" if has_cumsum else "parallel"),
        name=name,
    )(*operands)
    x_out = outs[0] if has_outproj else x2d
    rest = outs[1:] if has_outproj else outs
    return x_out, rest[0], (tuple(rest[1:]) if has_cumsum else rest[1])


def _split_w_in(w_in, n_main):
    d, n = w_in.shape
    w_main = w_in[:, :n_main].astype(BF16)
    w_ext = jnp.zeros((d, LANES), F32).at[:, :n - n_main].set(w_in[:, n_main:]).astype(BF16)
    return w_main, w_ext


def _outproj_kernel(g_ref, w_ref, x_ref, o_ref):
    o_ref[...] = x_ref[...] + _dot(g_ref[...], w_ref[...])


def _outproj(g2d, w_out, x2d, name):
    t, d = x2d.shape
    k = g2d.shape[1]
    tm = 1024
    return pl.pallas_call(
        _outproj_kernel,
        grid=(t // tm,),
        in_specs=[
            pl.BlockSpec((tm, k), lambda i: (i, 0)),
            pl.BlockSpec((k, d), lambda i: (0, 0)),
            pl.BlockSpec((tm, d), lambda i: (i, 0)),
        ],
        out_specs=pl.BlockSpec((tm, d), lambda i: (i, 0)),
        out_shape=jax.ShapeDtypeStruct((t, d), F32),
        compiler_params=_params("parallel"),
        name=name,
    )(g2d, w_out.astype(BF16), x2d)


def _fox_attn_kernel(q_ref, k_ref, v_ref, z_ref, ccol_ref, crow_ref, o_ref, *, tq, bq, tk):
    h = pl.program_id(1)
    i = pl.program_id(2)
    nsub = tq // bq
    lane = lax.broadcasted_iota(jnp.int32, (bq, LANES), 1)
    qs = [q_ref[0, r * bq:(r + 1) * bq, :] for r in range(nsub)]
    cqs = [jnp.sum(jnp.where(lane == h, ccol_ref[0, r * bq:(r + 1) * bq, :], 0.0), axis=-1, keepdims=True)
           for r in range(nsub)]
    row = lax.broadcasted_iota(jnp.int32, (bq, tk), 0)
    col = lax.broadcasted_iota(jnp.int32, (bq, tk), 1)

    def load_keys(start):
        return (k_ref[0, pl.ds(start, tk), :], v_ref[0, pl.ds(start, tk), :], crow_ref[0, pl.ds(h, 1), pl.ds(start, tk)])

    def attend(r, carry, keys, row_minus_col):
        m, l, acc = carry
        k, v, ck = keys
        t = _dot_nt(qs[r], k) - ck
        if row_minus_col is not None:
            t = jnp.where(row + row_minus_col >= col, t, -jnp.inf)
        m_new = jnp.maximum(m, jnp.max(t, axis=-1, keepdims=True) + cqs[r])
        alpha = jnp.exp2(m - m_new)
        p = jnp.exp2(t + (cqs[r] - m_new))
        l = alpha * l + jnp.sum(p, axis=-1, keepdims=True)
        acc = alpha * acc + _dot(p.astype(BF16), v)
        return m_new, l, acc

    def below_diagonal(kb, carries):
        keys = load_keys(pl.multiple_of(kb * tk, tk))
        return tuple(attend(r, carries[r], keys, None) for r in range(nsub))

    init = tuple((jnp.full((bq, 1), -jnp.inf, F32), jnp.zeros((bq, 1), F32), jnp.zeros((bq, LANES), F32))
                 for _ in range(nsub))
    carries = list(lax.fori_loop(0, i * (tq // tk), below_diagonal, init))
    for c in range(tq // tk):
        keys = load_keys(pl.multiple_of(i * tq + c * tk, tk))
        for r in range(nsub):
            row0, col0 = r * bq, c * tk
            if col0 > row0 + bq - 1:
                continue
            crosses = col0 + tk - 1 > row0
            carries[r] = attend(r, carries[r], keys, (row0 - col0) if crosses else None)
    for r in range(nsub):
        _, l, acc = carries[r]
        z = z_ref[0, r * bq:(r + 1) * bq, :].astype(F32)
        o_ref[0, r * bq:(r + 1) * bq, :] = (acc / l * _silu(z)).astype(o_ref.dtype)


def _fox_attn_bounded_kernel(q_ref, k_ref, v_ref, z_ref, ccol_ref, crow_ref, bound_ref, o_ref, *, tq, tk, unroll):
    h = pl.program_id(1)
    i = pl.program_id(2)
    nsub = tq // tk
    lane = lax.broadcasted_iota(jnp.int32, (tq, LANES), 1)
    cq = jnp.sum(jnp.where(lane == h, ccol_ref[0], 0.0), axis=-1, keepdims=True)
    dq = cq - bound_ref[0:1, 0:1]
    q = q_ref[0]
    ones = jnp.ones((tk, LANES), BF16)

    def weighted_values(start, first_row, diagonal):
        k = k_ref[0, pl.ds(start, tk), :]
        v1 = jnp.concatenate([v_ref[0, pl.ds(start, tk), :], ones], axis=1)
        ck = crow_ref[0, pl.ds(h, 1), pl.ds(start, tk)]
        e = (_dot_nt(q[first_row:], k) - ck) + dq[first_row:]
        if diagonal:
            row = lax.broadcasted_iota(jnp.int32, (tk, tk), 0)
            col = lax.broadcasted_iota(jnp.int32, (tk, tk), 1)
            top = jnp.where(row >= col, e[:tk], -jnp.inf)
            e = top if e.shape[0] == tk else jnp.concatenate([top, e[tk:]], axis=0)
        return _dot(jnp.exp2(e).astype(BF16), v1)

    def below_diagonal(jj, acc):
        for u in range(unroll):
            acc = acc + weighted_values(pl.multiple_of((jj * unroll + u) * tk, tk), 0, False)
        return acc

    acc = lax.fori_loop(0, i * nsub // unroll, below_diagonal, jnp.zeros((tq, 2 * LANES), F32))
    for c in range(nsub):
        pv = weighted_values(pl.multiple_of(i * tq + c * tk, tk), c * tk, True)
        acc = acc + pv if c == 0 else jnp.concatenate([acc[:c * tk], acc[c * tk:] + pv], axis=0)
    z = z_ref[0].astype(F32)
    o_ref[0] = (acc[:, :LANES] / acc[:, LANES:LANES + 1] * _silu(z)).astype(o_ref.dtype)


def _fox_attn(y, ccol, crow, logit_bound):
    b, s, _ = y.shape
    h = FOX_HEADS

    def specs(tq):
        return [
            pl.BlockSpec((1, tq, LANES), lambda bi, hi, i: (bi, i, hi)),
            pl.BlockSpec((1, s, LANES), lambda bi, hi, i: (bi, 0, h + hi)),
            pl.BlockSpec((1, s, LANES), lambda bi, hi, i: (bi, 0, 2 * h + hi)),
            pl.BlockSpec((1, tq, LANES), lambda bi, hi, i: (bi, i, 3 * h + hi)),
            pl.BlockSpec((1, tq, LANES), lambda bi, hi, i: (bi, i, 0)),
            pl.BlockSpec((1, h, s), lambda bi, hi, i: (bi, 0, 0)),
        ]

    def online():
        tq, bq, tk = min(FOX_TQ, s), FOX_BQ, FOX_TK
        return pl.pallas_call(
            functools.partial(_fox_attn_kernel, tq=tq, bq=bq, tk=tk),
            grid=(b, h, s // tq),
            in_specs=specs(tq),
            out_specs=pl.BlockSpec((1, tq, LANES), lambda bi, hi, i: (bi, i, hi)),
            out_shape=jax.ShapeDtypeStruct((b, s, FOX_WIDTH), BF16),
            compiler_params=_params("parallel", "parallel", "arbitrary"),
            name="fox_attn",
        )(y, y, y, y, ccol, crow)

    def bounded():
        tq, tk = min(FOX_BOUNDED_TQ, s), FOX_TK
        unroll = tq // tk
        return pl.pallas_call(
            functools.partial(_fox_attn_bounded_kernel, tq=tq, tk=tk, unroll=unroll),
            grid=(b, h, s // tq),
            in_specs=specs(tq) + [pl.BlockSpec((1, LANES), lambda bi, hi, i: (0, 0))],
            out_specs=pl.BlockSpec((1, tq, LANES), lambda bi, hi, i: (bi, i, hi)),
            out_shape=jax.ShapeDtypeStruct((b, s, FOX_WIDTH), BF16),
            compiler_params=_params("parallel", "parallel", "arbitrary"),
            name="fox_attn_bounded",
        )(y, y, y, y, ccol, crow, jnp.full((1, LANES), logit_bound, F32))

    return lax.cond(logit_bound <= FOX_MAX_FIXED_SHIFT, bounded, online)


def _fox_layer(x2d, prev, bsz, norm_w, w_in, b_f, q_gain, k_gain, tag):
    t, d = x2d.shape
    s = t // bsz
    w_main, w_ext = _split_w_in(w_in, 4 * FOX_WIDTH)
    scale = FOX_HEAD_DIM ** -0.5 * LOG2_E
    gain = jnp.concatenate([jnp.tile(q_gain * scale, FOX_HEADS), jnp.tile(k_gain, FOX_HEADS)]).reshape(1, -1)
    x2d, y, (ccol, crow) = _inproj(x2d, prev, norm_w, w_main, w_ext, gain, 2 * FOX_HEADS, BF16, "fox_front" + tag,
                                   forget_bias=b_f, seq_len=s)
    logit_bound = FOX_HEAD_DIM * jnp.max(jnp.abs(q_gain * scale)) * jnp.max(jnp.abs(k_gain)) * 1.02
    g = _fox_attn(y.reshape(bsz, s, -1), ccol.reshape(bsz, s, LANES), crow, logit_bound)
    return x2d, g.reshape(t, FOX_WIDTH)


def _gla_kernel(q_ref, k_ref, v_ref, z_ref, ext_ref, wup_ref, bg_ref, gain_ref, o_ref, state_ref, *, rows):
    chunk = GLA_CHUNK
    chunks = [slice(c * chunk, (c + 1) * chunk) for c in range(rows // chunk)]

    @pl.when(pl.program_id(2) == 0)
    def _():
        state_ref[...] = jnp.zeros_like(state_ref)

    gate = _dot(ext_ref[0].astype(BF16), wup_ref[...]) + bg_ref[...]
    log_a = _log_sigmoid(gate) * (1.0 / GLA_TAU)
    bcum = _chunk_cumsum(log_a, chunk)
    q = q_ref[0].astype(F32) * (GLA_KEY_DIM ** -0.5)
    k = k_ref[0].astype(F32)
    q_dec = (q * jnp.exp(bcum)).astype(BF16)
    k_inv = (k * jnp.exp(-bcum)).astype(BF16)
    row = lax.broadcasted_iota(jnp.int32, (chunk, chunk), 0)
    col = lax.broadcasted_iota(jnp.int32, (chunk, chunk), 1)
    causal = row >= col
    b_last = [bcum[sl][chunk - 1:chunk, :] for sl in chunks]
    k_dec = [(k[sl] * jnp.exp(b_last[c] - bcum[sl])).astype(BF16) for c, sl in enumerate(chunks)]
    v16 = [v_ref[0, sl, :] for sl in chunks]
    attn = [jnp.where(causal, _dot_nt(q_dec[sl], k_inv[sl]), 0.0).astype(BF16) for sl in chunks]
    update = [_dot(v16[c].astype(F32).T.astype(BF16), k_dec[c]) for c in range(len(chunks))]
    state_t = state_ref[...]
    s16 = []
    for c in range(len(chunks)):
        s16.append(state_t.astype(BF16))
        state_t = state_t * jnp.exp(b_last[c]) + update[c]
    state_ref[...] = state_t
    for c, sl in enumerate(chunks):
        o = _dot(attn[c], v16[c]) + _dot_nt(q_dec[sl], s16[c])
        ms = jnp.mean(o * o, axis=-1, keepdims=True)
        on = o * lax.rsqrt(ms + RMS_EPS) * gain_ref[...]
        o_ref[0, sl, :] = (on * _silu(z_ref[0, sl, :].astype(F32))).astype(o_ref.dtype)


def _gla_layer(x2d, prev, bsz, norm_w, w_in, w_gate_up, b_gate, o_gain):
    t, d = x2d.shape
    s = t // bsz
    h, dk, dv = GLA_HEADS, GLA_KEY_DIM, GLA_VAL_DIM
    n_main = 2 * h * dk + 2 * GLA_WIDTH
    w_main, w_ext = _split_w_in(w_in, n_main)
    x2d, y, ext = _inproj(x2d, prev, norm_w, w_main, w_ext, jnp.ones((1, LANES), F32), 0, BF16, "gla_front")
    y = y.reshape(bsz, s, n_main)
    ext = ext.reshape(bsz, s, LANES)
    wup = jnp.zeros((LANES, h * dk), F32).at[:GLA_RANK].set(w_gate_up).astype(BF16)
    rows = min(GLA_ROWS, s)
    kern = functools.partial(_gla_kernel, rows=rows)
    k_off = h * dk // dk
    v_off = 2 * h * dk // dv
    z_off = (2 * h * dk + GLA_WIDTH) // dv
    g = pl.pallas_call(
        kern,
        grid=(bsz, h, s // rows),
        in_specs=[
            pl.BlockSpec((1, rows, dk), lambda bi, hi, i: (bi, i, hi)),
            pl.BlockSpec((1, rows, dk), lambda bi, hi, i: (bi, i, k_off + hi)),
            pl.BlockSpec((1, rows, dv), lambda bi, hi, i: (bi, i, v_off + hi)),
            pl.BlockSpec((1, rows, dv), lambda bi, hi, i: (bi, i, z_off + hi)),
            pl.BlockSpec((1, rows, LANES), lambda bi, hi, i: (bi, i, 0)),
            pl.BlockSpec((LANES, dk), lambda bi, hi, i: (0, hi)),
            pl.BlockSpec((1, dk), lambda bi, hi, i: (0, hi)),
            pl.BlockSpec((1, dv), lambda bi, hi, i: (0, 0)),
        ],
        out_specs=pl.BlockSpec((1, rows, dv), lambda bi, hi, i: (bi, i, hi)),
        out_shape=jax.ShapeDtypeStruct((bsz, s, GLA_WIDTH), BF16),
        scratch_shapes=[pltpu.VMEM((dv, dk), F32)],
        compiler_params=_params("parallel", "parallel", "arbitrary"),
        name="gla_chunk",
    )(y, y, y, y, ext, wup, b_gate.reshape(1, -1), o_gain.reshape(1, -1))
    return x2d, g.reshape(t, GLA_WIDTH)


GDN_HISTORY = 8


def _gdn_conv_heads(u_ref, w_ref, hist_ref, rows):
    pad = GDN_HISTORY
    first = pl.program_id(1) == 0

    @pl.when(first)
    def _():
        hist_ref[0:pad, :] = jnp.zeros((pad, hist_ref.shape[1]), F32)

    @pl.when(jnp.logical_not(first))
    def _():
        hist_ref[0:pad, :] = hist_ref[rows:rows + pad, :]

    hist_ref[pad:pad + rows, :] = u_ref[0].astype(F32)
    heads = []
    for g in range(GDN_CONV_DIM // LANES):
        cols = slice(g * LANES, (g + 1) * LANES)
        acc = hist_ref[pad:pad + rows, cols] * w_ref[GDN_CONV - 1:GDN_CONV, cols]
        for tap in range(GDN_CONV - 1):
            off = pad - (GDN_CONV - 1) + tap
            acc = acc + hist_ref[off:off + rows, cols] * w_ref[tap:tap + 1, cols]
        seg = _silu(acc)
        if g < 2 * GDN_QK_HEADS:
            ss = jnp.sum(seg * seg, axis=-1, keepdims=True)
            seg = seg * lax.rsqrt(ss + RMS_EPS)
            if g < GDN_QK_HEADS:
                seg = seg * (GDN_HEAD_DIM ** -0.5)
        heads.append(seg)
    n_hq = GDN_QK_HEADS
    return heads[:n_hq], heads[n_hq:2 * n_hq], heads[2 * n_hq:]


def _gdn_kernel(u_ref, z_ref, ext_ref, cw_ref, alog_ref, dtb_ref, gain_ref, o_ref, state_ref, hist_ref, *, rows):
    chunk = GDN_CHUNK
    d = GDN_HEAD_DIM
    n_hq, n_hv = GDN_QK_HEADS, GDN_V_HEADS
    rep = n_hv // n_hq
    chunks = [slice(c * chunk, (c + 1) * chunk) for c in range(rows // chunk)]
    pairs = [(hv, c) for hv in range(n_hv) for c in range(len(chunks))]

    @pl.when(pl.program_id(1) == 0)
    def _():
        state_ref[...] = jnp.zeros_like(state_ref)

    q, k, v_heads = _gdn_conv_heads(u_ref, cw_ref, hist_ref, rows)

    ext = ext_ref[0]
    g_all = -jnp.exp(alog_ref[...]) * _softplus(ext + dtb_ref[...])
    gc_all = _chunk_cumsum(g_all, chunk)
    gc_all_t = gc_all.T
    beta_all = _sigmoid(ext)
    row = lax.broadcasted_iota(jnp.int32, (chunk, chunk), 0)
    col = lax.broadcasted_iota(jnp.int32, (chunk, chunk), 1)
    causal = row >= col
    strict = row > col

    q16 = [x.astype(BF16) for x in q]
    k16 = [x.astype(BF16) for x in k]
    kk = {(h, c): _dot_nt(k16[h][sl], k16[h][sl]) for h in range(n_hq) for c, sl in enumerate(chunks)}
    qk = {(h, c): _dot_nt(q16[h][sl], k16[h][sl]) for h in range(n_hq) for c, sl in enumerate(chunks)}

    gc, rhs, q_dec = [], [], []
    for hv in range(n_hv):
        h = hv // rep
        gc_h = gc_all[:, hv:hv + 1]
        beta = beta_all[:, n_hv + hv:n_hv + hv + 1]
        eg = jnp.exp(gc_h)
        v = v_heads[hv]
        gc.append(gc_h)
        rhs.append(jnp.concatenate([v * beta, k[h] * (beta * eg)], axis=-1))
        q_dec.append((q[h] * eg).astype(BF16))

    decay, nk, m = {}, {}, {}
    for hv, c in pairs:
        sl = chunks[c]
        diff = gc[hv][sl] - gc_all_t[hv:hv + 1, sl]
        decay[hv, c] = jnp.where(causal, jnp.exp(jnp.where(causal, diff, 0.0)), 0.0)
        beta = beta_all[sl, n_hv + hv:n_hv + hv + 1]
        nk[hv, c] = -jnp.where(strict, kk[hv // rep, c] * beta * decay[hv, c], 0.0)
        m[hv, c] = nk[hv, c]
    span = 2
    while span < chunk:
        n16 = {p: nk[p].astype(BF16) for p in pairs}
        nk = {p: _dot(n16[p], n16[p]) for p in pairs}
        mn = {p: _dot(m[p].astype(BF16), nk[p].astype(BF16)) for p in pairs}
        m = {p: m[p] + nk[p] + mn[p] for p in pairs}
        span *= 2
    uw = {(hv, c): rhs[hv][chunks[c]] + _dot(m[hv, c].astype(BF16), rhs[hv][chunks[c]].astype(BF16))
          for hv, c in pairs}

    heads = range(n_hv)
    state = [state_ref[hv] for hv in heads]
    for c, sl in enumerate(chunks):
        g_last = [gc[hv][sl][chunk - 1:chunk, :] for hv in heads]
        k_dec_t = [(k[hv // rep][sl] * jnp.exp(g_last[hv] - gc[hv][sl])).T.astype(BF16) for hv in heads]
        qk16 = [(qk[hv // rep, c] * decay[hv, c]).astype(BF16) for hv in heads]
        s16 = [state[hv].astype(BF16) for hv in heads]
        ws = [_dot(uw[hv, c][:, d:].astype(BF16), s16[hv]) for hv in heads]
        qs = [_dot(q_dec[hv][sl], s16[hv]) for hv in heads]
        v16 = [(uw[hv, c][:, :d] - ws[hv]).astype(BF16) for hv in heads]
        o = [qs[hv] + _dot(qk16[hv], v16[hv]) for hv in heads]
        state = [state[hv] * jnp.exp(g_last[hv]) + _dot(k_dec_t[hv], v16[hv]) for hv in heads]
        for hv in heads:
            ms = jnp.mean(o[hv] * o[hv], axis=-1, keepdims=True)
            on = o[hv] * lax.rsqrt(ms + RMS_EPS) * gain_ref[...]
            z = z_ref[0, sl, hv * d:(hv + 1) * d].astype(F32)
            o_ref[0, sl, hv * d:(hv + 1) * d] = (on * _silu(z)).astype(o_ref.dtype)
    for hv in heads:
        state_ref[hv] = state[hv]


def _gdn_layer(x2d, prev, bsz, norm_w, w_in, conv_w, a_log, dt_bias, o_gain):
    t, dm = x2d.shape
    s = t // bsz
    hv, d = GDN_V_HEADS, GDN_HEAD_DIM
    n_main = GDN_CONV_DIM + GDN_WIDTH
    w_main, w_ext = _split_w_in(w_in, n_main)
    x2d, y, ext = _inproj(x2d, prev, norm_w, w_main, w_ext, jnp.ones((1, LANES), F32), 0, BF16, "gdn_front")
    y = y.reshape(bsz, s, n_main)
    ext = ext.reshape(bsz, s, LANES)
    alog = jnp.zeros((1, LANES), F32).at[0, :hv].set(a_log)
    dtb = jnp.zeros((1, LANES), F32).at[0, :hv].set(dt_bias)
    rows = min(GDN_ROWS, s)
    kern = functools.partial(_gdn_kernel, rows=rows)
    g = pl.pallas_call(
        kern,
        grid=(bsz, s // rows),
        in_specs=[
            pl.BlockSpec((1, rows, GDN_CONV_DIM), lambda bi, i: (bi, i, 0)),
            pl.BlockSpec((1, rows, GDN_WIDTH), lambda bi, i: (bi, i, GDN_CONV_DIM // GDN_WIDTH)),
            pl.BlockSpec((1, rows, LANES), lambda bi, i: (bi, i, 0)),
            pl.BlockSpec((GDN_CONV, GDN_CONV_DIM), lambda bi, i: (0, 0)),
            pl.BlockSpec((1, LANES), lambda bi, i: (0, 0)),
            pl.BlockSpec((1, LANES), lambda bi, i: (0, 0)),
            pl.BlockSpec((1, d), lambda bi, i: (0, 0)),
        ],
        out_specs=pl.BlockSpec((1, rows, GDN_WIDTH), lambda bi, i: (bi, i, 0)),
        out_shape=jax.ShapeDtypeStruct((bsz, s, GDN_WIDTH), BF16),
        scratch_shapes=[pltpu.VMEM((hv, d, d), F32), pltpu.VMEM((GDN_HISTORY + rows, GDN_CONV_DIM), F32)],
        compiler_params=_params("parallel", "arbitrary"),
        name="gdn_chunk",
    )(y, y, ext, conv_w, alog, dtb, o_gain.reshape(1, -1))
    return x2d, g.reshape(t, GDN_WIDTH)


def kernel(x, norm_w, fox_w_in, fox_b_f, fox_q_gain, fox_k_gain, fox_w_out, gla_w_in, gla_w_gate_up, gla_b_gate, gla_o_gain, gla_w_out, gdn_w_in, gdn_conv_w, gdn_a_log, gdn_dt_bias, gdn_o_gain, gdn_w_out):
    bsz, s, d = x.shape
    x2d = x.reshape(bsz * s, d)
    prev = None
    for layer in range(DEPTH):
        kind, idx = layer % N_MIXERS, layer // N_MIXERS
        if kind == 0:
            x2d, g = _fox_layer(x2d, prev, bsz, norm_w[layer], fox_w_in[idx], fox_b_f[idx], fox_q_gain[idx],
                                fox_k_gain[idx], str(idx))
            prev = (g, fox_w_out[idx])
        elif kind == 1:
            x2d, g = _gla_layer(x2d, prev, bsz, norm_w[layer], gla_w_in[idx], gla_w_gate_up[idx], gla_b_gate[idx],
                                gla_o_gain[idx])
            prev = (g, gla_w_out[idx])
        else:
            x2d, g = _gdn_layer(x2d, prev, bsz, norm_w[layer], gdn_w_in[idx], gdn_conv_w[idx], gdn_a_log[idx],
                                gdn_dt_bias[idx], gdn_o_gain[idx])
            prev = (g, gdn_w_out[idx])
    x2d = _outproj(prev[0], prev[1], x2d, "outproj_last")
    return x2d.reshape(bsz, s, d)
```

```python
import functools
import math

import jax
import jax.numpy as jnp
from jax import lax
from jax.experimental import pallas as pl
from jax.experimental.pallas import tpu as pltpu

F32 = jnp.float32
BF16 = jnp.bfloat16

D_MODEL = 1024
DEPTH = 4
N_MIXERS = 3
RMS_EPS = 1e-6
LANES = 128

FOX_HEADS = 8
FOX_HEAD_DIM = 128
FOX_WIDTH = 1024

GLA_HEADS = 4
GLA_KEY_DIM = 128
GLA_VAL_DIM = 256
GLA_WIDTH = 1024
GLA_RANK = 16
GLA_TAU = 16.0
GLA_CHUNK = 64

GDN_QK_HEADS = 4
GDN_V_HEADS = 8
GDN_HEAD_DIM = 128
GDN_WIDTH = 1024
GDN_CONV = 4
GDN_CHUNK = 64
GDN_CONV_DIM = 2048

VMEM_LIMIT_BYTES = 56 * 1024 * 1024

LOG2_E = math.log2(math.e)

FOX_TQ, FOX_BQ, FOX_TK = 1024, 512, 512
FOX_BOUNDED_TQ = 2048
FOX_MAX_FIXED_SHIFT = 40.0
GLA_ROWS = 2048
GDN_ROWS = 256
PROJ_ROWS, PROJ_COLS = 1024, 512


def _params(*semantics):
    return pltpu.CompilerParams(dimension_semantics=semantics, vmem_limit_bytes=VMEM_LIMIT_BYTES)


def _dot(a, b):
    return jnp.dot(a, b, preferred_element_type=F32)


def _dot_nt(a, b):
    return lax.dot_general(a, b, (((1,), (1,)), ((), ())), preferred_element_type=F32)


def _sigmoid(x):
    return 1.0 / (1.0 + jnp.exp2(x * (-LOG2_E)))


def _silu(x):
    return x * _sigmoid(x)


def _softplus(x):
    return jnp.maximum(x, 0.0) + jnp.log(1.0 + jnp.exp(-jnp.abs(x)))


def _log_sigmoid(x):
    return -_softplus(-x)


def _chunk_cumsum(x, chunk):
    row = lax.broadcasted_iota(jnp.int32, x.shape, 0) & (chunk - 1)
    shift = 1
    while shift < chunk:
        x = x + jnp.where(row >= shift, pltpu.roll(x, shift, axis=0), 0.0)
        shift *= 2
    return x


def _inproj_kernel(*refs, tn, n_norm_groups, has_outproj, has_cumsum, tiles_per_seq):
    it = iter(refs)
    x_ref = next(it)
    g_ref, wo_ref = (next(it), next(it)) if has_outproj else (None, None)
    nw_ref, w_ref, we_ref, gain_ref = next(it), next(it), next(it), next(it)
    bias_ref = next(it) if has_cumsum else None
    xo_ref = next(it) if has_outproj else None
    o_ref = next(it)

    tm = x_ref.shape[0]
    xn = []
    for r in (slice(0, tm // 2), slice(tm // 2, tm)):
        x = x_ref[r, :]
        if has_outproj:
            x = x + _dot(g_ref[r, :], wo_ref[...])
            xo_ref[r, :] = x
        ms = jnp.mean(x * x, axis=-1, keepdims=True)
        xn.append((x * lax.rsqrt(ms + RMS_EPS) * nw_ref[...]).astype(BF16))
    xn = jnp.concatenate(xn, axis=0)
    extra = _dot(xn, we_ref[...])
    if has_cumsum:
        ccol_ref, crow_ref, carry_ref = next(it), next(it), next(it)

        @pl.when(pl.program_id(0) % tiles_per_seq == 0)
        def _():
            carry_ref[...] = jnp.zeros_like(carry_ref)

        log_f = _log_sigmoid(extra + bias_ref[...]) * LOG2_E
        tm = log_f.shape[0]
        cs = _chunk_cumsum(log_f, tm) + carry_ref[...]
        carry_ref[...] = cs[tm - 1:tm, :]
        ccol_ref[...] = cs
        crow_ref[0] = cs.T[:FOX_HEADS, :]
    else:
        next(it)[...] = extra
    n = o_ref.shape[1]
    for j in range(n // tn):
        acc = _dot(xn, w_ref[:, j * tn:(j + 1) * tn])
        for g in range(tn // LANES):
            col = j * tn + g * LANES
            seg = acc[:, g * LANES:(g + 1) * LANES]
            if col // LANES < n_norm_groups:
                msq = jnp.mean(seg * seg, axis=-1, keepdims=True)
                seg = seg * lax.rsqrt(msq + RMS_EPS) * gain_ref[:, col:col + LANES]
            o_ref[:, col:col + LANES] = seg.astype(o_ref.dtype)


def _inproj(x2d, prev, norm_w, w_main, w_ext, gain, n_norm_groups, out_dtype, name, forget_bias=None, seq_len=None):
    t, d = x2d.shape
    n = w_main.shape[1]
    tm, tn = PROJ_ROWS, PROJ_COLS
    has_outproj = prev is not None
    has_cumsum = forget_bias is not None
    row = lambda i: (i, 0)
    fixed = lambda i: (0, 0)
    resident = pl.Buffered(1)
    operands, in_specs = [x2d], [pl.BlockSpec((tm, d), row)]
    if has_outproj:
        g2d, w_out = prev
        operands += [g2d, w_out.astype(BF16)]
        in_specs += [pl.BlockSpec((tm, g2d.shape[1]), row), pl.BlockSpec(w_out.shape, fixed, pipeline_mode=resident)]
    operands += [norm_w.reshape(1, d), w_main, w_ext, gain]
    in_specs += [pl.BlockSpec((1, d), fixed), pl.BlockSpec((d, n), fixed, pipeline_mode=resident),
                 pl.BlockSpec((d, LANES), fixed),
                 pl.BlockSpec((1, gain.shape[1]), fixed)]
    out_shape, out_specs, scratch = [], [], []
    if has_outproj:
        out_shape.append(jax.ShapeDtypeStruct((t, d), F32))
        out_specs.append(pl.BlockSpec((tm, d), row))
    out_shape.append(jax.ShapeDtypeStruct((t, n), out_dtype))
    out_specs.append(pl.BlockSpec((tm, n), row))
    out_shape.append(jax.ShapeDtypeStruct((t, LANES), F32))
    out_specs.append(pl.BlockSpec((tm, LANES), row))
    tiles_per_seq = None
    if has_cumsum:
        tiles_per_seq = seq_len // tm
        operands += [jnp.zeros((1, LANES), F32).at[0, :FOX_HEADS].set(forget_bias)]
        in_specs += [pl.BlockSpec((1, LANES), fixed)]
        out_shape.append(jax.ShapeDtypeStruct((t // seq_len, FOX_HEADS, seq_len), F32))
        out_specs.append(pl.BlockSpec((1, FOX_HEADS, tm), lambda i: (i // tiles_per_seq, 0, i % tiles_per_seq)))
        scratch.append(pltpu.VMEM((1, LANES), F32))
    kern = functools.partial(_inproj_kernel, tn=tn, n_norm_groups=n_norm_groups, has_outproj=has_outproj,
                             has_cumsum=has_cumsum, tiles_per_seq=tiles_per_seq)
    outs = pl.pallas_call(
        kern,
        grid=(t // tm,),
        in_specs=in_specs,
        out_specs=out_specs,
        out_shape=out_shape,
        scratch_shapes=scratch,
        compiler_params=_params("arbitrary" if has_cumsum else "parallel"),
        name=name,
    )(*operands)
    x_out = outs[0] if has_outproj else x2d
    rest = outs[1:] if has_outproj else outs
    return x_out, rest[0], (tuple(rest[1:]) if has_cumsum else rest[1])


def _split_w_in(w_in, n_main):
    d, n = w_in.shape
    w_main = w_in[:, :n_main].astype(BF16)
    w_ext = jnp.zeros((d, LANES), F32).at[:, :n - n_main].set(w_in[:, n_main:]).astype(BF16)
    return w_main, w_ext


def _outproj_kernel(g_ref, w_ref, x_ref, o_ref):
    o_ref[...] = x_ref[...] + _dot(g_ref[...], w_ref[...])


def _outproj(g2d, w_out, x2d, name):
    t, d = x2d.shape
    k = g2d.shape[1]
    tm = PROJ_ROWS
    return pl.pallas_call(
        _outproj_kernel,
        grid=(t // tm,),
        in_specs=[
            pl.BlockSpec((tm, k), lambda i: (i, 0)),
            pl.BlockSpec((k, d), lambda i: (0, 0)),
            pl.BlockSpec((tm, d), lambda i: (i, 0)),
        ],
        out_specs=pl.BlockSpec((tm, d), lambda i: (i, 0)),
        out_shape=jax.ShapeDtypeStruct((t, d), F32),
        compiler_params=_params("parallel"),
        name=name,
    )(g2d, w_out.astype(BF16), x2d)


def _fox_attn_kernel(q_ref, k_ref, v_ref, z_ref, ccol_ref, crow_ref, o_ref, *, tq, bq, tk):
    h = pl.program_id(1)
    i = pl.program_id(2)
    nsub = tq // bq
    lane = lax.broadcasted_iota(jnp.int32, (bq, LANES), 1)
    qs = [q_ref[0, r * bq:(r + 1) * bq, :] for r in range(nsub)]
    cqs = [jnp.sum(jnp.where(lane == h, ccol_ref[0, r * bq:(r + 1) * bq, :], 0.0), axis=-1, keepdims=True)
           for r in range(nsub)]
    row = lax.broadcasted_iota(jnp.int32, (bq, tk), 0)
    col = lax.broadcasted_iota(jnp.int32, (bq, tk), 1)

    def load_keys(start):
        return (k_ref[0, pl.ds(start, tk), :], v_ref[0, pl.ds(start, tk), :], crow_ref[0, pl.ds(h, 1), pl.ds(start, tk)])

    def attend(r, carry, keys, row_minus_col):
        m, l, acc = carry
        k, v, ck = keys
        t = _dot_nt(qs[r], k) - ck
        if row_minus_col is not None:
            t = jnp.where(row + row_minus_col >= col, t, -jnp.inf)
        m_new = jnp.maximum(m, jnp.max(t, axis=-1, keepdims=True) + cqs[r])
        alpha = jnp.exp2(m - m_new)
        p = jnp.exp2(t + (cqs[r] - m_new))
        l = alpha * l + jnp.sum(p, axis=-1, keepdims=True)
        acc = alpha * acc + _dot(p.astype(BF16), v)
        return m_new, l, acc

    def below_diagonal(kb, carries):
        keys = load_keys(pl.multiple_of(kb * tk, tk))
        return tuple(attend(r, carries[r], keys, None) for r in range(nsub))

    init = tuple((jnp.full((bq, 1), -jnp.inf, F32), jnp.zeros((bq, 1), F32), jnp.zeros((bq, LANES), F32))
                 for _ in range(nsub))
    carries = list(lax.fori_loop(0, i * (tq // tk), below_diagonal, init))
    for c in range(tq // tk):
        keys = load_keys(pl.multiple_of(i * tq + c * tk, tk))
        for r in range(nsub):
            row0, col0 = r * bq, c * tk
            if col0 > row0 + bq - 1:
                continue
            crosses = col0 + tk - 1 > row0
            carries[r] = attend(r, carries[r], keys, (row0 - col0) if crosses else None)
    for r in range(nsub):
        _, l, acc = carries[r]
        z = z_ref[0, r * bq:(r + 1) * bq, :].astype(F32)
        o_ref[0, r * bq:(r + 1) * bq, :] = (acc / l * _silu(z)).astype(o_ref.dtype)


def _fox_attn_bounded_kernel(q_ref, k_ref, v_ref, z_ref, ccol_ref, crow_ref, bound_ref, o_ref, *, tq, tk, unroll):
    h = pl.program_id(1)
    i = pl.program_id(2)
    nsub = tq // tk
    lane = lax.broadcasted_iota(jnp.int32, (tq, LANES), 1)
    cq = jnp.sum(jnp.where(lane == h, ccol_ref[0], 0.0), axis=-1, keepdims=True)
    dq = cq - bound_ref[0:1, 0:1]
    q = q_ref[0]
    ones = jnp.ones((tk, LANES), BF16)

    def weighted_values(start, first_row, diagonal):
        k = k_ref[0, pl.ds(start, tk), :]
        v1 = jnp.concatenate([v_ref[0, pl.ds(start, tk), :], ones], axis=1)
        ck = crow_ref[0, pl.ds(h, 1), pl.ds(start, tk)]
        e = (_dot_nt(q[first_row:], k) - ck) + dq[first_row:]
        if diagonal:
            row = lax.broadcasted_iota(jnp.int32, (tk, tk), 0)
            col = lax.broadcasted_iota(jnp.int32, (tk, tk), 1)
            top = jnp.where(row >= col, e[:tk], -jnp.inf)
            e = top if e.shape[0] == tk else jnp.concatenate([top, e[tk:]], axis=0)
        return _dot(jnp.exp2(e).astype(BF16), v1)

    def below_diagonal(jj, acc):
        for u in range(unroll):
            acc = acc + weighted_values(pl.multiple_of((jj * unroll + u) * tk, tk), 0, False)
        return acc

    acc = lax.fori_loop(0, i * nsub // unroll, below_diagonal, jnp.zeros((tq, 2 * LANES), F32))
    for c in range(nsub):
        pv = weighted_values(pl.multiple_of(i * tq + c * tk, tk), c * tk, True)
        acc = acc + pv if c == 0 else jnp.concatenate([acc[:c * tk], acc[c * tk:] + pv], axis=0)
    z = z_ref[0].astype(F32)
    o_ref[0] = (acc[:, :LANES] / acc[:, LANES:LANES + 1] * _silu(z)).astype(o_ref.dtype)


def _fox_attn(y, ccol, crow, logit_bound):
    b, s, _ = y.shape
    h = FOX_HEADS

    def specs(tq):
        return [
            pl.BlockSpec((1, tq, LANES), lambda bi, hi, i: (bi, i, hi)),
            pl.BlockSpec((1, s, LANES), lambda bi, hi, i: (bi, 0, h + hi)),
            pl.BlockSpec((1, s, LANES), lambda bi, hi, i: (bi, 0, 2 * h + hi)),
            pl.BlockSpec((1, tq, LANES), lambda bi, hi, i: (bi, i, 3 * h + hi)),
            pl.BlockSpec((1, tq, LANES), lambda bi, hi, i: (bi, i, 0)),
            pl.BlockSpec((1, h, s), lambda bi, hi, i: (bi, 0, 0)),
        ]

    def online():
        tq, bq, tk = min(FOX_TQ, s), FOX_BQ, FOX_TK
        return pl.pallas_call(
            functools.partial(_fox_attn_kernel, tq=tq, bq=bq, tk=tk),
            grid=(b, h, s // tq),
            in_specs=specs(tq),
            out_specs=pl.BlockSpec((1, tq, LANES), lambda bi, hi, i: (bi, i, hi)),
            out_shape=jax.ShapeDtypeStruct((b, s, FOX_WIDTH), BF16),
            compiler_params=_params("parallel", "parallel", "arbitrary"),
            name="fox_attn",
        )(y, y, y, y, ccol, crow)

    def bounded():
        tq, tk = min(FOX_BOUNDED_TQ, s), FOX_TK
        unroll = tq // tk
        return pl.pallas_call(
            functools.partial(_fox_attn_bounded_kernel, tq=tq, tk=tk, unroll=unroll),
            grid=(b, h, s // tq),
            in_specs=specs(tq) + [pl.BlockSpec((1, LANES), lambda bi, hi, i: (0, 0))],
            out_specs=pl.BlockSpec((1, tq, LANES), lambda bi, hi, i: (bi, i, hi)),
            out_shape=jax.ShapeDtypeStruct((b, s, FOX_WIDTH), BF16),
            compiler_params=_params("parallel", "parallel", "arbitrary"),
            name="fox_attn_bounded",
        )(y, y, y, y, ccol, crow, jnp.full((1, LANES), logit_bound, F32))

    return lax.cond(logit_bound <= FOX_MAX_FIXED_SHIFT, bounded, online)


def _fox_layer(x2d, prev, bsz, norm_w, w_in, b_f, q_gain, k_gain, tag):
    t, d = x2d.shape
    s = t // bsz
    w_main, w_ext = _split_w_in(w_in, 4 * FOX_WIDTH)
    scale = FOX_HEAD_DIM ** -0.5 * LOG2_E
    gain = jnp.concatenate([jnp.tile(q_gain * scale, FOX_HEADS), jnp.tile(k_gain, FOX_HEADS)]).reshape(1, -1)
    x2d, y, (ccol, crow) = _inproj(x2d, prev, norm_w, w_main, w_ext, gain, 2 * FOX_HEADS, BF16, "fox_front" + tag,
                                   forget_bias=b_f, seq_len=s)
    logit_bound = FOX_HEAD_DIM * jnp.max(jnp.abs(q_gain * scale)) * jnp.max(jnp.abs(k_gain)) * 1.02
    g = _fox_attn(y.reshape(bsz, s, -1), ccol.reshape(bsz, s, LANES), crow, logit_bound)
    return x2d, g.reshape(t, FOX_WIDTH)


def _gla_kernel(q_ref, k_ref, v_ref, z_ref, ext_ref, wup_ref, bg_ref, gain_ref, o_ref, state_ref, *, rows):
    chunk = GLA_CHUNK
    chunks = [slice(c * chunk, (c + 1) * chunk) for c in range(rows // chunk)]

    @pl.when(pl.program_id(2) == 0)
    def _():
        state_ref[...] = jnp.zeros_like(state_ref)

    gate = _dot(ext_ref[0].astype(BF16), wup_ref[...]) + bg_ref[...]
    log_a = _log_sigmoid(gate) * (1.0 / GLA_TAU)
    bcum = _chunk_cumsum(log_a, chunk)
    q = q_ref[0].astype(F32) * (GLA_KEY_DIM ** -0.5)
    k = k_ref[0].astype(F32)
    q_dec = (q * jnp.exp(bcum)).astype(BF16)
    k_inv = (k * jnp.exp(-bcum)).astype(BF16)
    row = lax.broadcasted_iota(jnp.int32, (chunk, chunk), 0)
    col = lax.broadcasted_iota(jnp.int32, (chunk, chunk), 1)
    causal = row >= col
    b_last = [bcum[sl][chunk - 1:chunk, :] for sl in chunks]
    k_dec = [(k[sl] * jnp.exp(b_last[c] - bcum[sl])).astype(BF16) for c, sl in enumerate(chunks)]
    v16 = [v_ref[0, sl, :] for sl in chunks]
    attn = [jnp.where(causal, _dot_nt(q_dec[sl], k_inv[sl]), 0.0).astype(BF16) for sl in chunks]
    update = [_dot(v16[c].astype(F32).T.astype(BF16), k_dec[c]) for c in range(len(chunks))]
    state_t = state_ref[...]
    s16 = []
    for c in range(len(chunks)):
        s16.append(state_t.astype(BF16))
        state_t = state_t * jnp.exp(b_last[c]) + update[c]
    state_ref[...] = state_t
    for c, sl in enumerate(chunks):
        o = _dot(attn[c], v16[c]) + _dot_nt(q_dec[sl], s16[c])
        ms = jnp.mean(o * o, axis=-1, keepdims=True)
        on = o * lax.rsqrt(ms + RMS_EPS) * gain_ref[...]
        o_ref[0, sl, :] = (on * _silu(z_ref[0, sl, :].astype(F32))).astype(o_ref.dtype)


def _gla_layer(x2d, prev, bsz, norm_w, w_in, w_gate_up, b_gate, o_gain):
    t, d = x2d.shape
    s = t // bsz
    h, dk, dv = GLA_HEADS, GLA_KEY_DIM, GLA_VAL_DIM
    n_main = 2 * h * dk + 2 * GLA_WIDTH
    w_main, w_ext = _split_w_in(w_in, n_main)
    x2d, y, ext = _inproj(x2d, prev, norm_w, w_main, w_ext, jnp.ones((1, LANES), F32), 0, BF16, "gla_front")
    y = y.reshape(bsz, s, n_main)
    ext = ext.reshape(bsz, s, LANES)
    wup = jnp.zeros((LANES, h * dk), F32).at[:GLA_RANK].set(w_gate_up).astype(BF16)
    rows = min(GLA_ROWS, s)
    kern = functools.partial(_gla_kernel, rows=rows)
    k_off = h * dk // dk
    v_off = 2 * h * dk // dv
    z_off = (2 * h * dk + GLA_WIDTH) // dv
    g = pl.pallas_call(
        kern,
        grid=(bsz, h, s // rows),
        in_specs=[
            pl.BlockSpec((1, rows, dk), lambda bi, hi, i: (bi, i, hi)),
            pl.BlockSpec((1, rows, dk), lambda bi, hi, i: (bi, i, k_off + hi)),
            pl.BlockSpec((1, rows, dv), lambda bi, hi, i: (bi, i, v_off + hi)),
            pl.BlockSpec((1, rows, dv), lambda bi, hi, i: (bi, i, z_off + hi)),
            pl.BlockSpec((1, rows, LANES), lambda bi, hi, i: (bi, i, 0)),
            pl.BlockSpec((LANES, dk), lambda bi, hi, i: (0, hi)),
            pl.BlockSpec((1, dk), lambda bi, hi, i: (0, hi)),
            pl.BlockSpec((1, dv), lambda bi, hi, i: (0, 0)),
        ],
        out_specs=pl.BlockSpec((1, rows, dv), lambda bi, hi, i: (bi, i, hi)),
        out_shape=jax.ShapeDtypeStruct((bsz, s, GLA_WIDTH), BF16),
        scratch_shapes=[pltpu.VMEM((dv, dk), F32)],
        compiler_params=_params("parallel", "parallel", "arbitrary"),
        name="gla_chunk",
    )(y, y, y, y, ext, wup, b_gate.reshape(1, -1), o_gain.reshape(1, -1))
    return x2d, g.reshape(t, GLA_WIDTH)


GDN_HISTORY = 8


def _gdn_conv_heads(u_ref, w_ref, hist_ref, rows):
    pad = GDN_HISTORY
    first = pl.program_id(1) == 0

    @pl.when(first)
    def _():
        hist_ref[0:pad, :] = jnp.zeros((pad, hist_ref.shape[1]), F32)

    @pl.when(jnp.logical_not(first))
    def _():
        hist_ref[0:pad, :] = hist_ref[rows:rows + pad, :]

    hist_ref[pad:pad + rows, :] = u_ref[0].astype(F32)
    heads = []
    for g in range(GDN_CONV_DIM // LANES):
        cols = slice(g * LANES, (g + 1) * LANES)
        acc = hist_ref[pad:pad + rows, cols] * w_ref[GDN_CONV - 1:GDN_CONV, cols]
        for tap in range(GDN_CONV - 1):
            off = pad - (GDN_CONV - 1) + tap
            acc = acc + hist_ref[off:off + rows, cols] * w_ref[tap:tap + 1, cols]
        seg = _silu(acc)
        if g < 2 * GDN_QK_HEADS:
            ss = jnp.sum(seg * seg, axis=-1, keepdims=True)
            seg = seg * lax.rsqrt(ss + RMS_EPS)
            if g < GDN_QK_HEADS:
                seg = seg * (GDN_HEAD_DIM ** -0.5)
        heads.append(seg)
    n_hq = GDN_QK_HEADS
    return heads[:n_hq], heads[n_hq:2 * n_hq], heads[2 * n_hq:]


def _gdn_kernel(u_ref, z_ref, ext_ref, cw_ref, alog_ref, dtb_ref, gain_ref, o_ref, state_ref, hist_ref, *, rows):
    chunk = GDN_CHUNK
    d = GDN_HEAD_DIM
    n_hq, n_hv = GDN_QK_HEADS, GDN_V_HEADS
    rep = n_hv // n_hq
    chunks = [slice(c * chunk, (c + 1) * chunk) for c in range(rows // chunk)]
    pairs = [(hv, c) for hv in range(n_hv) for c in range(len(chunks))]

    @pl.when(pl.program_id(1) == 0)
    def _():
        state_ref[...] = jnp.zeros_like(state_ref)

    q, k, v_heads = _gdn_conv_heads(u_ref, cw_ref, hist_ref, rows)

    ext = ext_ref[0]
    g_all = -jnp.exp(alog_ref[...]) * _softplus(ext + dtb_ref[...])
    gc_all = _chunk_cumsum(g_all, chunk)
    gc_all_t = gc_all.T
    beta_all = _sigmoid(ext)
    row = lax.broadcasted_iota(jnp.int32, (chunk, chunk), 0)
    col = lax.broadcasted_iota(jnp.int32, (chunk, chunk), 1)
    causal = row >= col
    strict = row > col

    q16 = [x.astype(BF16) for x in q]
    k16 = [x.astype(BF16) for x in k]
    kk = {(h, c): _dot_nt(k16[h][sl], k16[h][sl]) for h in range(n_hq) for c, sl in enumerate(chunks)}
    qk = {(h, c): _dot_nt(q16[h][sl], k16[h][sl]) for h in range(n_hq) for c, sl in enumerate(chunks)}

    gc, rhs, q_dec = [], [], []
    for hv in range(n_hv):
        h = hv // rep
        gc_h = gc_all[:, hv:hv + 1]
        beta = beta_all[:, n_hv + hv:n_hv + hv + 1]
        eg = jnp.exp(gc_h)
        v = v_heads[hv]
        gc.append(gc_h)
        rhs.append(jnp.concatenate([v * beta, k[h] * (beta * eg)], axis=-1))
        q_dec.append((q[h] * eg).astype(BF16))

    decay, nk, m = {}, {}, {}
    for hv, c in pairs:
        sl = chunks[c]
        diff = gc[hv][sl] - gc_all_t[hv:hv + 1, sl]
        decay[hv, c] = jnp.where(causal, jnp.exp(jnp.where(causal, diff, 0.0)), 0.0)
        beta = beta_all[sl, n_hv + hv:n_hv + hv + 1]
        nk[hv, c] = -jnp.where(strict, kk[hv // rep, c] * beta * decay[hv, c], 0.0)
        m[hv, c] = nk[hv, c]
    span = 2
    while span < chunk:
        n16 = {p: nk[p].astype(BF16) for p in pairs}
        nk = {p: _dot(n16[p], n16[p]) for p in pairs}
        mn = {p: _dot(m[p].astype(BF16), nk[p].astype(BF16)) for p in pairs}
        m = {p: m[p] + nk[p] + mn[p] for p in pairs}
        span *= 2
    uw = {(hv, c): rhs[hv][chunks[c]] + _dot(m[hv, c].astype(BF16), rhs[hv][chunks[c]].astype(BF16))
          for hv, c in pairs}

    heads = range(n_hv)
    state = [state_ref[hv] for hv in heads]
    for c, sl in enumerate(chunks):
        g_last = [gc[hv][sl][chunk - 1:chunk, :] for hv in heads]
        k_dec_t = [(k[hv // rep][sl] * jnp.exp(g_last[hv] - gc[hv][sl])).T.astype(BF16) for hv in heads]
        qk16 = [(qk[hv // rep, c] * decay[hv, c]).astype(BF16) for hv in heads]
        s16 = [state[hv].astype(BF16) for hv in heads]
        ws = [_dot(uw[hv, c][:, d:].astype(BF16), s16[hv]) for hv in heads]
        qs = [_dot(q_dec[hv][sl], s16[hv]) for hv in heads]
        v16 = [(uw[hv, c][:, :d] - ws[hv]).astype(BF16) for hv in heads]
        o = [qs[hv] + _dot(qk16[hv], v16[hv]) for hv in heads]
        state = [state[hv] * jnp.exp(g_last[hv]) + _dot(k_dec_t[hv], v16[hv]) for hv in heads]
        for hv in heads:
            ms = jnp.mean(o[hv] * o[hv], axis=-1, keepdims=True)
            on = o[hv] * lax.rsqrt(ms + RMS_EPS) * gain_ref[...]
            z = z_ref[0, sl, hv * d:(hv + 1) * d].astype(F32)
            o_ref[0, sl, hv * d:(hv + 1) * d] = (on * _silu(z)).astype(o_ref.dtype)
    for hv in heads:
        state_ref[hv] = state[hv]


def _gdn_layer(x2d, prev, bsz, norm_w, w_in, conv_w, a_log, dt_bias, o_gain):
    t, dm = x2d.shape
    s = t // bsz
    hv, d = GDN_V_HEADS, GDN_HEAD_DIM
    n_main = GDN_CONV_DIM + GDN_WIDTH
    w_main, w_ext = _split_w_in(w_in, n_main)
    x2d, y, ext = _inproj(x2d, prev, norm_w, w_main, w_ext, jnp.ones((1, LANES), F32), 0, BF16, "gdn_front")
    y = y.reshape(bsz, s, n_main)
    ext = ext.reshape(bsz, s, LANES)
    alog = jnp.zeros((1, LANES), F32).at[0, :hv].set(a_log)
    dtb = jnp.zeros((1, LANES), F32).at[0, :hv].set(dt_bias)
    rows = min(GDN_ROWS, s)
    kern = functools.partial(_gdn_kernel, rows=rows)
    g = pl.pallas_call(
        kern,
        grid=(bsz, s // rows),
        in_specs=[
            pl.BlockSpec((1, rows, GDN_CONV_DIM), lambda bi, i: (bi, i, 0)),
            pl.BlockSpec((1, rows, GDN_WIDTH), lambda bi, i: (bi, i, GDN_CONV_DIM // GDN_WIDTH)),
            pl.BlockSpec((1, rows, LANES), lambda bi, i: (bi, i, 0)),
            pl.BlockSpec((GDN_CONV, GDN_CONV_DIM), lambda bi, i: (0, 0)),
            pl.BlockSpec((1, LANES), lambda bi, i: (0, 0)),
            pl.BlockSpec((1, LANES), lambda bi, i: (0, 0)),
            pl.BlockSpec((1, d), lambda bi, i: (0, 0)),
        ],
        out_specs=pl.BlockSpec((1, rows, GDN_WIDTH), lambda bi, i: (bi, i, 0)),
        out_shape=jax.ShapeDtypeStruct((bsz, s, GDN_WIDTH), BF16),
        scratch_shapes=[pltpu.VMEM((hv, d, d), F32), pltpu.VMEM((GDN_HISTORY + rows, GDN_CONV_DIM), F32)],
        compiler_params=_params("parallel", "arbitrary"),
        name="gdn_chunk",
    )(y, y, ext, conv_w, alog, dtb, o_gain.reshape(1, -1))
    return x2d, g.reshape(t, GDN_WIDTH)


def kernel(x, norm_w, fox_w_in, fox_b_f, fox_q_gain, fox_k_gain, fox_w_out, gla_w_in, gla_w_gate_up, gla_b_gate, gla_o_gain, gla_w_out, gdn_w_in, gdn_conv_w, gdn_a_log, gdn_dt_bias, gdn_o_gain, gdn_w_out):
    bsz, s, d = x.shape
    x2d = x.reshape(bsz * s, d)
    prev = None
    for layer in range(DEPTH):
        kind, idx = layer % N_MIXERS, layer // N_MIXERS
        if kind == 0:
            x2d, g = _fox_layer(x2d, prev, bsz, norm_w[layer], fox_w_in[idx], fox_b_f[idx], fox_q_gain[idx],
                                fox_k_gain[idx], str(idx))
            prev = (g, fox_w_out[idx])
        elif kind == 1:
            x2d, g = _gla_layer(x2d, prev, bsz, norm_w[layer], gla_w_in[idx], gla_w_gate_up[idx], gla_b_gate[idx],
                                gla_o_gain[idx])
            prev = (g, gla_w_out[idx])
        else:
            x2d, g = _gdn_layer(x2d, prev, bsz, norm_w[layer], gdn_w_in[idx], gdn_conv_w[idx], gdn_a_log[idx],
                                gdn_dt_bias[idx], gdn_o_gain[idx])
            prev = (g, gdn_w_out[idx])
    x2d = _outproj(prev[0], prev[1], x2d, "outproj_last")
    return x2d.reshape(bsz, s, d)
```

```python
import functools
import math

import jax
import jax.numpy as jnp
from jax import lax
from jax.experimental import pallas as pl
from jax.experimental.pallas import tpu as pltpu

F32 = jnp.float32
BF16 = jnp.bfloat16

D_MODEL = 1024
DEPTH = 4
N_MIXERS = 3
RMS_EPS = 1e-6
LANES = 128

FOX_HEADS = 8
FOX_HEAD_DIM = 128
FOX_WIDTH = 1024

GLA_HEADS = 4
GLA_KEY_DIM = 128
GLA_VAL_DIM = 256
GLA_WIDTH = 1024
GLA_RANK = 16
GLA_TAU = 16.0
GLA_CHUNK = 64

GDN_QK_HEADS = 4
GDN_V_HEADS = 8
GDN_HEAD_DIM = 128
GDN_WIDTH = 1024
GDN_CONV = 4
GDN_CHUNK = 64
GDN_CONV_DIM = 2048

VMEM_LIMIT_BYTES = 56 * 1024 * 1024

LOG2_E = math.log2(math.e)

FOX_TQ, FOX_BQ, FOX_TK = 1024, 512, 512
FOX_BOUNDED_TQ = 2048
FOX_MAX_FIXED_SHIFT = 40.0
GLA_ROWS = 2048
GDN_ROWS = 256
PROJ_ROWS, PROJ_COLS = 1024, 512


def _params(*semantics):
    return pltpu.CompilerParams(dimension_semantics=semantics, vmem_limit_bytes=VMEM_LIMIT_BYTES)


def _dot(a, b):
    return jnp.dot(a, b, preferred_element_type=F32)


def _dot_nt(a, b):
    return lax.dot_general(a, b, (((1,), (1,)), ((), ())), preferred_element_type=F32)


def _sigmoid(x):
    return 1.0 / (1.0 + jnp.exp2(x * (-LOG2_E)))


def _silu(x):
    return x * _sigmoid(x)


def _softplus(x):
    return jnp.maximum(x, 0.0) + jnp.log(1.0 + jnp.exp(-jnp.abs(x)))


def _log_sigmoid(x):
    return -_softplus(-x)


def _chunk_cumsum(x, chunk):
    row = lax.broadcasted_iota(jnp.int32, x.shape, 0) & (chunk - 1)
    shift = 1
    while shift < chunk:
        x = x + jnp.where(row >= shift, pltpu.roll(x, shift, axis=0), 0.0)
        shift *= 2
    return x


def _inproj_kernel(*refs, tn, n_norm_groups, has_outproj, has_cumsum, tiles_per_seq):
    it = iter(refs)
    x_ref = next(it)
    g_ref, wo_ref = (next(it), next(it)) if has_outproj else (None, None)
    nw_ref, w_ref, we_ref, gain_ref = next(it), next(it), next(it), next(it)
    bias_ref = next(it) if has_cumsum else None
    xo_ref = next(it) if has_outproj else None
    o_ref = next(it)

    tm = x_ref.shape[0]
    xn = []
    for r in (slice(0, tm // 2), slice(tm // 2, tm)):
        x = x_ref[r, :]
        if has_outproj:
            x = x + _dot(g_ref[r, :], wo_ref[...])
            xo_ref[r, :] = x
        ms = jnp.mean(x * x, axis=-1, keepdims=True)
        xn.append((x * lax.rsqrt(ms + RMS_EPS) * nw_ref[...]).astype(BF16))
    xn = jnp.concatenate(xn, axis=0)
    extra = _dot(xn, we_ref[...])
    if has_cumsum:
        ccol_ref, crow_ref, carry_ref = next(it), next(it), next(it)

        @pl.when(pl.program_id(0) % tiles_per_seq == 0)
        def _():
            carry_ref[...] = jnp.zeros_like(carry_ref)

        log_f = _log_sigmoid(extra + bias_ref[...]) * LOG2_E
        tm = log_f.shape[0]
        cs = _chunk_cumsum(log_f, tm) + carry_ref[...]
        carry_ref[...] = cs[tm - 1:tm, :]
        ccol_ref[...] = cs
        crow_ref[0] = cs.T[:FOX_HEADS, :]
    else:
        next(it)[...] = extra
    n = o_ref.shape[1]
    for j in range(n // tn):
        acc = _dot(xn, w_ref[:, j * tn:(j + 1) * tn])
        for g in range(tn // LANES):
            col = j * tn + g * LANES
            seg = acc[:, g * LANES:(g + 1) * LANES]
            if col // LANES < n_norm_groups:
                msq = jnp.mean(seg * seg, axis=-1, keepdims=True)
                seg = seg * lax.rsqrt(msq + RMS_EPS) * gain_ref[:, col:col + LANES]
            o_ref[:, col:col + LANES] = seg.astype(o_ref.dtype)


def _inproj(x2d, prev, norm_w, w_main, w_ext, gain, n_norm_groups, out_dtype, name, forget_bias=None, seq_len=None):
    t, d = x2d.shape
    n = w_main.shape[1]
    tm, tn = PROJ_ROWS, PROJ_COLS
    has_outproj = prev is not None
    has_cumsum = forget_bias is not None
    row = lambda i: (i, 0)
    fixed = lambda i: (0, 0)
    resident = pl.Buffered(1)
    operands, in_specs = [x2d], [pl.BlockSpec((tm, d), row)]
    if has_outproj:
        g2d, w_out = prev
        operands += [g2d, w_out.astype(BF16)]
        in_specs += [pl.BlockSpec((tm, g2d.shape[1]), row), pl.BlockSpec(w_out.shape, fixed, pipeline_mode=resident)]
    operands += [norm_w.reshape(1, d), w_main, w_ext, gain]
    in_specs += [pl.BlockSpec((1, d), fixed), pl.BlockSpec((d, n), fixed, pipeline_mode=resident),
                 pl.BlockSpec((d, LANES), fixed),
                 pl.BlockSpec((1, gain.shape[1]), fixed)]
    out_shape, out_specs, scratch = [], [], []
    if has_outproj:
        out_shape.append(jax.ShapeDtypeStruct((t, d), F32))
        out_specs.append(pl.BlockSpec((tm, d), row))
    out_shape.append(jax.ShapeDtypeStruct((t, n), out_dtype))
    out_specs.append(pl.BlockSpec((tm, n), row))
    out_shape.append(jax.ShapeDtypeStruct((t, LANES), F32))
    out_specs.append(pl.BlockSpec((tm, LANES), row))
    tiles_per_seq = None
    if has_cumsum:
        tiles_per_seq = seq_len // tm
        operands += [jnp.zeros((1, LANES), F32).at[0, :FOX_HEADS].set(forget_bias)]
        in_specs += [pl.BlockSpec((1, LANES), fixed)]
        out_shape.append(jax.ShapeDtypeStruct((t // seq_len, FOX_HEADS, seq_len), F32))
        out_specs.append(pl.BlockSpec((1, FOX_HEADS, tm), lambda i: (i // tiles_per_seq, 0, i % tiles_per_seq)))
        scratch.append(pltpu.VMEM((1, LANES), F32))
    kern = functools.partial(_inproj_kernel, tn=tn, n_norm_groups=n_norm_groups, has_outproj=has_outproj,
                             has_cumsum=has_cumsum, tiles_per_seq=tiles_per_seq)
    outs = pl.pallas_call(
        kern,
        grid=(t // tm,),
        in_specs=in_specs,
        out_specs=out_specs,
        out_shape=out_shape,
        scratch_shapes=scratch,
        compiler_params=_params("arbitrary" if has_cumsum else "parallel"),
        name=name,
    )(*operands)
    x_out = outs[0] if has_outproj else x2d
    rest = outs[1:] if has_outproj else outs
    return x_out, rest[0], (tuple(rest[1:]) if has_cumsum else rest[1])


def _split_w_in(w_in, n_main):
    d, n = w_in.shape
    w_main = w_in[:, :n_main].astype(BF16)
    w_ext = jnp.zeros((d, LANES), F32).at[:, :n - n_main].set(w_in[:, n_main:]).astype(BF16)
    return w_main, w_ext


def _outproj_kernel(g_ref, w_ref, x_ref, o_ref):
    o_ref[...] = x_ref[...] + _dot(g_ref[...], w_ref[...])


def _outproj(g2d, w_out, x2d, name):
    t, d = x2d.shape
    k = g2d.shape[1]
    tm = PROJ_ROWS
    return pl.pallas_call(
        _outproj_kernel,
        grid=(t // tm,),
        in_specs=[
            pl.BlockSpec((tm, k), lambda i: (i, 0)),
            pl.BlockSpec((k, d), lambda i: (0, 0)),
            pl.BlockSpec((tm, d), lambda i: (i, 0)),
        ],
        out_specs=pl.BlockSpec((tm, d), lambda i: (i, 0)),
        out_shape=jax.ShapeDtypeStruct((t, d), F32),
        compiler_params=_params("parallel"),
        name=name,
    )(g2d, w_out.astype(BF16), x2d)


def _fox_attn_kernel(q_ref, k_ref, v_ref, z_ref, ccol_ref, crow_ref, o_ref, *, tq, bq, tk):
    h = pl.program_id(1)
    i = pl.program_id(2)
    nsub = tq // bq
    lane = lax.broadcasted_iota(jnp.int32, (bq, LANES), 1)
    qs = [q_ref[0, r * bq:(r + 1) * bq, :] for r in range(nsub)]
    cqs = [jnp.sum(jnp.where(lane == h, ccol_ref[0, r * bq:(r + 1) * bq, :], 0.0), axis=-1, keepdims=True)
           for r in range(nsub)]
    row = lax.broadcasted_iota(jnp.int32, (bq, tk), 0)
    col = lax.broadcasted_iota(jnp.int32, (bq, tk), 1)

    def load_keys(start):
        return (k_ref[0, pl.ds(start, tk), :], v_ref[0, pl.ds(start, tk), :], crow_ref[0, pl.ds(h, 1), pl.ds(start, tk)])

    def attend(r, carry, keys, row_minus_col):
        m, l, acc = carry
        k, v, ck = keys
        t = _dot_nt(qs[r], k) - ck
        if row_minus_col is not None:
            t = jnp.where(row + row_minus_col >= col, t, -jnp.inf)
        m_new = jnp.maximum(m, jnp.max(t, axis=-1, keepdims=True) + cqs[r])
        alpha = jnp.exp2(m - m_new)
        p = jnp.exp2(t + (cqs[r] - m_new))
        l = alpha * l + jnp.sum(p, axis=-1, keepdims=True)
        acc = alpha * acc + _dot(p.astype(BF16), v)
        return m_new, l, acc

    def below_diagonal(kb, carries):
        keys = load_keys(pl.multiple_of(kb * tk, tk))
        return tuple(attend(r, carries[r], keys, None) for r in range(nsub))

    init = tuple((jnp.full((bq, 1), -jnp.inf, F32), jnp.zeros((bq, 1), F32), jnp.zeros((bq, LANES), F32))
                 for _ in range(nsub))
    carries = list(lax.fori_loop(0, i * (tq // tk), below_diagonal, init))
    for c in range(tq // tk):
        keys = load_keys(pl.multiple_of(i * tq + c * tk, tk))
        for r in range(nsub):
            row0, col0 = r * bq, c * tk
            if col0 > row0 + bq - 1:
                continue
            crosses = col0 + tk - 1 > row0
            carries[r] = attend(r, carries[r], keys, (row0 - col0) if crosses else None)
    for r in range(nsub):
        _, l, acc = carries[r]
        z = z_ref[0, r * bq:(r + 1) * bq, :].astype(F32)
        o_ref[0, r * bq:(r + 1) * bq, :] = (acc / l * _silu(z)).astype(o_ref.dtype)


def _fox_attn_bounded_kernel(q_ref, k_ref, v_ref, z_ref, ccol_ref, crow_ref, bound_ref, o_ref, *, tq, tk, unroll):
    h = pl.program_id(1)
    i = pl.program_id(2)
    nsub = tq // tk
    lane = lax.broadcasted_iota(jnp.int32, (tq, LANES), 1)
    cq = jnp.sum(jnp.where(lane == h, ccol_ref[0], 0.0), axis=-1, keepdims=True)
    dq = cq - bound_ref[0:1, 0:1]
    q = q_ref[0]
    ones = jnp.ones((tk, LANES), BF16)

    def weighted_values(start, first_row, diagonal):
        k = k_ref[0, pl.ds(start, tk), :]
        v1 = jnp.concatenate([v_ref[0, pl.ds(start, tk), :], ones], axis=1)
        ck = crow_ref[0, pl.ds(h, 1), pl.ds(start, tk)]
        e = (_dot_nt(q[first_row:], k) - ck) + dq[first_row:]
        if diagonal:
            row = lax.broadcasted_iota(jnp.int32, (tk, tk), 0)
            col = lax.broadcasted_iota(jnp.int32, (tk, tk), 1)
            top = jnp.where(row >= col, e[:tk], -jnp.inf)
            e = top if e.shape[0] == tk else jnp.concatenate([top, e[tk:]], axis=0)
        return _dot(jnp.exp2(e).astype(BF16), v1)

    def below_diagonal(jj, acc):
        for u in range(unroll):
            acc = acc + weighted_values(pl.multiple_of((jj * unroll + u) * tk, tk), 0, False)
        return acc

    acc = lax.fori_loop(0, i * nsub // unroll, below_diagonal, jnp.zeros((tq, 2 * LANES), F32))
    for c in range(nsub):
        pv = weighted_values(pl.multiple_of(i * tq + c * tk, tk), c * tk, True)
        acc = acc + pv if c == 0 else jnp.concatenate([acc[:c * tk], acc[c * tk:] + pv], axis=0)
    z = z_ref[0].astype(F32)
    o_ref[0] = (acc[:, :LANES] / acc[:, LANES:LANES + 1] * _silu(z)).astype(o_ref.dtype)


def _fox_attn(y, ccol, crow, logit_bound):
    b, s, _ = y.shape
    h = FOX_HEADS

    def specs(tq):
        return [
            pl.BlockSpec((1, tq, LANES), lambda bi, hi, i: (bi, i, hi)),
            pl.BlockSpec((1, s, LANES), lambda bi, hi, i: (bi, 0, h + hi)),
            pl.BlockSpec((1, s, LANES), lambda bi, hi, i: (bi, 0, 2 * h + hi)),
            pl.BlockSpec((1, tq, LANES), lambda bi, hi, i: (bi, i, 3 * h + hi)),
            pl.BlockSpec((1, tq, LANES), lambda bi, hi, i: (bi, i, 0)),
            pl.BlockSpec((1, h, s), lambda bi, hi, i: (bi, 0, 0)),
        ]

    def online():
        tq, bq, tk = min(FOX_TQ, s), FOX_BQ, FOX_TK
        return pl.pallas_call(
            functools.partial(_fox_attn_kernel, tq=tq, bq=bq, tk=tk),
            grid=(b, h, s // tq),
            in_specs=specs(tq),
            out_specs=pl.BlockSpec((1, tq, LANES), lambda bi, hi, i: (bi, i, hi)),
            out_shape=jax.ShapeDtypeStruct((b, s, FOX_WIDTH), BF16),
            compiler_params=_params("parallel", "parallel", "arbitrary"),
            name="fox_attn",
        )(y, y, y, y, ccol, crow)

    def bounded():
        tq, tk = min(FOX_BOUNDED_TQ, s), FOX_TK
        unroll = tq // tk
        return pl.pallas_call(
            functools.partial(_fox_attn_bounded_kernel, tq=tq, tk=tk, unroll=unroll),
            grid=(b, h, s // tq),
            in_specs=specs(tq) + [pl.BlockSpec((1, LANES), lambda bi, hi, i: (0, 0))],
            out_specs=pl.BlockSpec((1, tq, LANES), lambda bi, hi, i: (bi, i, hi)),
            out_shape=jax.ShapeDtypeStruct((b, s, FOX_WIDTH), BF16),
            compiler_params=_params("parallel", "parallel", "arbitrary"),
            name="fox_attn_bounded",
        )(y, y, y, y, ccol, crow, jnp.full((1, LANES), logit_bound, F32))

    return lax.cond(logit_bound <= FOX_MAX_FIXED_SHIFT, bounded, online)


def _fox_layer(x2d, prev, bsz, norm_w, w_in, b_f, q_gain, k_gain, tag):
    t, d = x2d.shape
    s = t // bsz
    w_main, w_ext = _split_w_in(w_in, 4 * FOX_WIDTH)
    scale = FOX_HEAD_DIM ** -0.5 * LOG2_E
    gain = jnp.concatenate([jnp.tile(q_gain * scale, FOX_HEADS), jnp.tile(k_gain, FOX_HEADS)]).reshape(1, -1)
    x2d, y, (ccol, crow) = _inproj(x2d, prev, norm_w, w_main, w_ext, gain, 2 * FOX_HEADS, BF16, "fox_front" + tag,
                                   forget_bias=b_f, seq_len=s)
    logit_bound = FOX_HEAD_DIM * jnp.max(jnp.abs(q_gain * scale)) * jnp.max(jnp.abs(k_gain)) * 1.02
    g = _fox_attn(y.reshape(bsz, s, -1), ccol.reshape(bsz, s, LANES), crow, logit_bound)
    return x2d, g.reshape(t, FOX_WIDTH)


def _gla_kernel(q_ref, k_ref, v_ref, z_ref, ext_ref, wup_ref, bg_ref, gain_ref, o_ref, state_ref, *, rows):
    chunk = GLA_CHUNK
    chunks = [slice(c * chunk, (c + 1) * chunk) for c in range(rows // chunk)]

    @pl.when(pl.program_id(2) == 0)
    def _():
        state_ref[...] = jnp.zeros_like(state_ref)

    gate = _dot(ext_ref[0].astype(BF16), wup_ref[...]) + bg_ref[...]
    log_a = _log_sigmoid(gate) * (1.0 / GLA_TAU)
    bcum = _chunk_cumsum(log_a, chunk)
    q = q_ref[0].astype(F32) * (GLA_KEY_DIM ** -0.5)
    k = k_ref[0].astype(F32)
    q_dec = (q * jnp.exp(bcum)).astype(BF16)
    k_inv = (k * jnp.exp(-bcum)).astype(BF16)
    row = lax.broadcasted_iota(jnp.int32, (chunk, chunk), 0)
    col = lax.broadcasted_iota(jnp.int32, (chunk, chunk), 1)
    causal = row >= col
    b_last = [bcum[sl][chunk - 1:chunk, :] for sl in chunks]
    k_dec = [(k[sl] * jnp.exp(b_last[c] - bcum[sl])).astype(BF16) for c, sl in enumerate(chunks)]
    v16 = [v_ref[0, sl, :] for sl in chunks]
    attn = [jnp.where(causal, _dot_nt(q_dec[sl], k_inv[sl]), 0.0).astype(BF16) for sl in chunks]
    update = [_dot(v16[c].astype(F32).T.astype(BF16), k_dec[c]) for c in range(len(chunks))]
    state_t = state_ref[...]
    s16 = []
    for c in range(len(chunks)):
        s16.append(state_t.astype(BF16))
        state_t = state_t * jnp.exp(b_last[c]) + update[c]
    state_ref[...] = state_t
    for c, sl in enumerate(chunks):
        o = _dot(attn[c], v16[c]) + _dot_nt(q_dec[sl], s16[c])
        ms = jnp.mean(o * o, axis=-1, keepdims=True)
        on = o * lax.rsqrt(ms + RMS_EPS) * gain_ref[...]
        o_ref[0, sl, :] = (on * _silu(z_ref[0, sl, :].astype(F32))).astype(o_ref.dtype)


def _gla_layer(x2d, prev, bsz, norm_w, w_in, w_gate_up, b_gate, o_gain):
    t, d = x2d.shape
    s = t // bsz
    h, dk, dv = GLA_HEADS, GLA_KEY_DIM, GLA_VAL_DIM
    n_main = 2 * h * dk + 2 * GLA_WIDTH
    w_main, w_ext = _split_w_in(w_in, n_main)
    x2d, y, ext = _inproj(x2d, prev, norm_w, w_main, w_ext, jnp.ones((1, LANES), F32), 0, BF16, "gla_front")
    y = y.reshape(bsz, s, n_main)
    ext = ext.reshape(bsz, s, LANES)
    wup = jnp.zeros((LANES, h * dk), F32).at[:GLA_RANK].set(w_gate_up).astype(BF16)
    rows = min(GLA_ROWS, s)
    kern = functools.partial(_gla_kernel, rows=rows)
    k_off = h * dk // dk
    v_off = 2 * h * dk // dv
    z_off = (2 * h * dk + GLA_WIDTH) // dv
    g = pl.pallas_call(
        kern,
        grid=(bsz, h, s // rows),
        in_specs=[
            pl.BlockSpec((1, rows, dk), lambda bi, hi, i: (bi, i, hi)),
            pl.BlockSpec((1, rows, dk), lambda bi, hi, i: (bi, i, k_off + hi)),
            pl.BlockSpec((1, rows, dv), lambda bi, hi, i: (bi, i, v_off + hi)),
            pl.BlockSpec((1, rows, dv), lambda bi, hi, i: (bi, i, z_off + hi)),
            pl.BlockSpec((1, rows, LANES), lambda bi, hi, i: (bi, i, 0)),
            pl.BlockSpec((LANES, dk), lambda bi, hi, i: (0, hi)),
            pl.BlockSpec((1, dk), lambda bi, hi, i: (0, hi)),
            pl.BlockSpec((1, dv), lambda bi, hi, i: (0, 0)),
        ],
        out_specs=pl.BlockSpec((1, rows, dv), lambda bi, hi, i: (bi, i, hi)),
        out_shape=jax.ShapeDtypeStruct((bsz, s, GLA_WIDTH), BF16),
        scratch_shapes=[pltpu.VMEM((dv, dk), F32)],
        compiler_params=_params("parallel", "parallel", "arbitrary"),
        name="gla_chunk",
    )(y, y, y, y, ext, wup, b_gate.reshape(1, -1), o_gain.reshape(1, -1))
    return x2d, g.reshape(t, GLA_WIDTH)


GDN_HISTORY = 8


def _gdn_conv_heads(u_ref, w_ref, hist_ref, rows):
    pad = GDN_HISTORY
    first = pl.program_id(1) == 0

    @pl.when(first)
    def _():
        hist_ref[0:pad, :] = jnp.zeros((pad, hist_ref.shape[1]), F32)

    @pl.when(jnp.logical_not(first))
    def _():
        hist_ref[0:pad, :] = hist_ref[rows:rows + pad, :]

    hist_ref[pad:pad + rows, :] = u_ref[0].astype(F32)
    heads = []
    for g in range(GDN_CONV_DIM // LANES):
        cols = slice(g * LANES, (g + 1) * LANES)
        acc = hist_ref[pad:pad + rows, cols] * w_ref[GDN_CONV - 1:GDN_CONV, cols]
        for tap in range(GDN_CONV - 1):
            off = pad - (GDN_CONV - 1) + tap
            acc = acc + hist_ref[off:off + rows, cols] * w_ref[tap:tap + 1, cols]
        seg = _silu(acc)
        if g < 2 * GDN_QK_HEADS:
            ss = jnp.sum(seg * seg, axis=-1, keepdims=True)
            seg = seg * lax.rsqrt(ss + RMS_EPS)
            if g < GDN_QK_HEADS:
                seg = seg * (GDN_HEAD_DIM ** -0.5)
        heads.append(seg)
    n_hq = GDN_QK_HEADS
    return heads[:n_hq], heads[n_hq:2 * n_hq], heads[2 * n_hq:]


def _gdn_kernel(u_ref, z_ref, ext_ref, cw_ref, alog_ref, dtb_ref, gain_ref, o_ref, state_ref, hist_ref, *, rows):
    chunk = GDN_CHUNK
    d = GDN_HEAD_DIM
    n_hq, n_hv = GDN_QK_HEADS, GDN_V_HEADS
    rep = n_hv // n_hq
    chunks = [slice(c * chunk, (c + 1) * chunk) for c in range(rows // chunk)]
    pairs = [(hv, c) for hv in range(n_hv) for c in range(len(chunks))]

    @pl.when(pl.program_id(1) == 0)
    def _():
        state_ref[...] = jnp.zeros_like(state_ref)

    q, k, v_heads = _gdn_conv_heads(u_ref, cw_ref, hist_ref, rows)

    ext = ext_ref[0]
    g_all = -jnp.exp(alog_ref[...]) * _softplus(ext + dtb_ref[...])
    gc_all = _chunk_cumsum(g_all, chunk)
    gc_all_t = gc_all.T
    beta_all = _sigmoid(ext)
    row = lax.broadcasted_iota(jnp.int32, (chunk, chunk), 0)
    col = lax.broadcasted_iota(jnp.int32, (chunk, chunk), 1)
    causal = row >= col
    strict = row > col

    q16 = [x.astype(BF16) for x in q]
    k16 = [x.astype(BF16) for x in k]
    kk = {(h, c): _dot_nt(k16[h][sl], k16[h][sl]) for h in range(n_hq) for c, sl in enumerate(chunks)}
    qk = {(h, c): _dot_nt(q16[h][sl], k16[h][sl]) for h in range(n_hq) for c, sl in enumerate(chunks)}

    gc, beta_b, rhs, q_dec = [], [], [], []
    for hv in range(n_hv):
        h = hv // rep
        gc_h = jnp.broadcast_to(gc_all[:, hv:hv + 1], (rows, LANES))
        beta = jnp.broadcast_to(beta_all[:, n_hv + hv:n_hv + hv + 1], (rows, LANES))
        eg = jnp.exp(gc_h)
        v = v_heads[hv]
        gc.append(gc_h)
        beta_b.append(beta)
        rhs.append(jnp.concatenate([v * beta, k[h] * (beta * eg)], axis=-1))
        q_dec.append((q[h] * eg).astype(BF16))

    decay, nk, m = {}, {}, {}
    for hv, c in pairs:
        sl = chunks[c]
        diff = gc[hv][sl, :chunk] - gc_all_t[hv:hv + 1, sl]
        decay[hv, c] = jnp.where(causal, jnp.exp(jnp.where(causal, diff, 0.0)), 0.0)
        beta = beta_b[hv][sl, :chunk]
        nk[hv, c] = -jnp.where(strict, kk[hv // rep, c] * beta * decay[hv, c], 0.0)
        m[hv, c] = nk[hv, c]
    span = 2
    while span < chunk:
        n16 = {p: nk[p].astype(BF16) for p in pairs}
        nk = {p: _dot(n16[p], n16[p]) for p in pairs}
        mn = {p: _dot(m[p].astype(BF16), nk[p].astype(BF16)) for p in pairs}
        m = {p: m[p] + nk[p] + mn[p] for p in pairs}
        span *= 2
    uw = {(hv, c): rhs[hv][chunks[c]] + _dot(m[hv, c].astype(BF16), rhs[hv][chunks[c]].astype(BF16))
          for hv, c in pairs}

    heads = range(n_hv)
    state = [state_ref[hv] for hv in heads]
    for c, sl in enumerate(chunks):
        g_last = [gc[hv][sl][chunk - 1:chunk, :] for hv in heads]
        k_dec_t = [(k[hv // rep][sl] * jnp.exp(g_last[hv] - gc[hv][sl])).T.astype(BF16) for hv in heads]
        qk16 = [(qk[hv // rep, c] * decay[hv, c]).astype(BF16) for hv in heads]
        s16 = [state[hv].astype(BF16) for hv in heads]
        ws = [_dot(uw[hv, c][:, d:].astype(BF16), s16[hv]) for hv in heads]
        qs = [_dot(q_dec[hv][sl], s16[hv]) for hv in heads]
        v16 = [(uw[hv, c][:, :d] - ws[hv]).astype(BF16) for hv in heads]
        o = [qs[hv] + _dot(qk16[hv], v16[hv]) for hv in heads]
        state = [state[hv] * jnp.exp(g_last[hv]) + _dot(k_dec_t[hv], v16[hv]) for hv in heads]
        for hv in heads:
            ms = jnp.mean(o[hv] * o[hv], axis=-1, keepdims=True)
            on = o[hv] * lax.rsqrt(ms + RMS_EPS) * gain_ref[...]
            z = z_ref[0, sl, hv * d:(hv + 1) * d].astype(F32)
            o_ref[0, sl, hv * d:(hv + 1) * d] = (on * _silu(z)).astype(o_ref.dtype)
    for hv in heads:
        state_ref[hv] = state[hv]


def _gdn_layer(x2d, prev, bsz, norm_w, w_in, conv_w, a_log, dt_bias, o_gain):
    t, dm = x2d.shape
    s = t // bsz
    hv, d = GDN_V_HEADS, GDN_HEAD_DIM
    n_main = GDN_CONV_DIM + GDN_WIDTH
    w_main, w_ext = _split_w_in(w_in, n_main)
    x2d, y, ext = _inproj(x2d, prev, norm_w, w_main, w_ext, jnp.ones((1, LANES), F32), 0, BF16, "gdn_front")
    y = y.reshape(bsz, s, n_main)
    ext = ext.reshape(bsz, s, LANES)
    alog = jnp.zeros((1, LANES), F32).at[0, :hv].set(a_log)
    dtb = jnp.zeros((1, LANES), F32).at[0, :hv].set(dt_bias)
    rows = min(GDN_ROWS, s)
    kern = functools.partial(_gdn_kernel, rows=rows)
    g = pl.pallas_call(
        kern,
        grid=(bsz, s // rows),
        in_specs=[
            pl.BlockSpec((1, rows, GDN_CONV_DIM), lambda bi, i: (bi, i, 0)),
            pl.BlockSpec((1, rows, GDN_WIDTH), lambda bi, i: (bi, i, GDN_CONV_DIM // GDN_WIDTH)),
            pl.BlockSpec((1, rows, LANES), lambda bi, i: (bi, i, 0)),
            pl.BlockSpec((GDN_CONV, GDN_CONV_DIM), lambda bi, i: (0, 0)),
            pl.BlockSpec((1, LANES), lambda bi, i: (0, 0)),
            pl.BlockSpec((1, LANES), lambda bi, i: (0, 0)),
            pl.BlockSpec((1, d), lambda bi, i: (0, 0)),
        ],
        out_specs=pl.BlockSpec((1, rows, GDN_WIDTH), lambda bi, i: (bi, i, 0)),
        out_shape=jax.ShapeDtypeStruct((bsz, s, GDN_WIDTH), BF16),
        scratch_shapes=[pltpu.VMEM((hv, d, d), F32), pltpu.VMEM((GDN_HISTORY + rows, GDN_CONV_DIM), F32)],
        compiler_params=_params("parallel", "arbitrary"),
        name="gdn_chunk",
    )(y, y, ext, conv_w, alog, dtb, o_gain.reshape(1, -1))
    return x2d, g.reshape(t, GDN_WIDTH)


def kernel(x, norm_w, fox_w_in, fox_b_f, fox_q_gain, fox_k_gain, fox_w_out, gla_w_in, gla_w_gate_up, gla_b_gate, gla_o_gain, gla_w_out, gdn_w_in, gdn_conv_w, gdn_a_log, gdn_dt_bias, gdn_o_gain, gdn_w_out):
    bsz, s, d = x.shape
    x2d = x.reshape(bsz * s, d)
    prev = None
    for layer in range(DEPTH):
        kind, idx = layer % N_MIXERS, layer // N_MIXERS
        if kind == 0:
            x2d, g = _fox_layer(x2d, prev, bsz, norm_w[layer], fox_w_in[idx], fox_b_f[idx], fox_q_gain[idx],
                                fox_k_gain[idx], str(idx))
            prev = (g, fox_w_out[idx])
        elif kind == 1:
            x2d, g = _gla_layer(x2d, prev, bsz, norm_w[layer], gla_w_in[idx], gla_w_gate_up[idx], gla_b_gate[idx],
                                gla_o_gain[idx])
            prev = (g, gla_w_out[idx])
        else:
            x2d, g = _gdn_layer(x2d, prev, bsz, norm_w[layer], gdn_w_in[idx], gdn_conv_w[idx], gdn_a_log[idx],
                                gdn_dt_bias[idx], gdn_o_gain[idx])
            prev = (g, gdn_w_out[idx])
    x2d = _outproj(prev[0], prev[1], x2d, "outproj_last")
    return x2d.reshape(bsz, s, d)
```

```python
import functools
import math

import jax
import jax.numpy as jnp
from jax import lax
from jax.experimental import pallas as pl
from jax.experimental.pallas import tpu as pltpu

F32 = jnp.float32
BF16 = jnp.bfloat16

D_MODEL = 1024
DEPTH = 4
N_MIXERS = 3
RMS_EPS = 1e-6
LANES = 128

FOX_HEADS = 8
FOX_HEAD_DIM = 128
FOX_WIDTH = 1024

GLA_HEADS = 4
GLA_KEY_DIM = 128
GLA_VAL_DIM = 256
GLA_WIDTH = 1024
GLA_RANK = 16
GLA_TAU = 16.0
GLA_CHUNK = 64

GDN_QK_HEADS = 4
GDN_V_HEADS = 8
GDN_HEAD_DIM = 128
GDN_WIDTH = 1024
GDN_CONV = 4
GDN_CHUNK = 64
GDN_CONV_DIM = 2048

VMEM_LIMIT_BYTES = 56 * 1024 * 1024

LOG2_E = math.log2(math.e)

FOX_TQ, FOX_BQ, FOX_TK = 1024, 512, 512
FOX_BOUNDED_TQ = 2048
FOX_MAX_FIXED_SHIFT = 40.0
GLA_ROWS = 2048
GDN_ROWS = 256
PROJ_ROWS, PROJ_COLS = 1024, 512


def _params(*semantics):
    return pltpu.CompilerParams(dimension_semantics=semantics, vmem_limit_bytes=VMEM_LIMIT_BYTES)


def _dot(a, b):
    return jnp.dot(a, b, preferred_element_type=F32)


def _dot_nt(a, b):
    return lax.dot_general(a, b, (((1,), (1,)), ((), ())), preferred_element_type=F32)


def _sigmoid(x):
    return 1.0 / (1.0 + jnp.exp2(x * (-LOG2_E)))


def _silu(x):
    return x * _sigmoid(x)


def _softplus(x):
    return jnp.maximum(x, 0.0) + jnp.log(1.0 + jnp.exp(-jnp.abs(x)))


def _log_sigmoid(x):
    return -_softplus(-x)


def _chunk_cumsum(x, chunk):
    row = lax.broadcasted_iota(jnp.int32, x.shape, 0) & (chunk - 1)
    shift = 1
    while shift < chunk:
        x = x + jnp.where(row >= shift, pltpu.roll(x, shift, axis=0), 0.0)
        shift *= 2
    return x


def _inproj_kernel(*refs, tn, n_norm_groups, has_outproj, has_cumsum, tiles_per_seq):
    it = iter(refs)
    x_ref = next(it)
    g_ref, wo_ref = (next(it), next(it)) if has_outproj else (None, None)
    nw_ref, w_ref, we_ref, gain_ref = next(it), next(it), next(it), next(it)
    bias_ref = next(it) if has_cumsum else None
    xo_ref = next(it) if has_outproj else None
    o_ref = next(it)

    tm = x_ref.shape[0]
    xn = []
    for r in (slice(0, tm // 2), slice(tm // 2, tm)):
        x = x_ref[r, :]
        if has_outproj:
            x = x + _dot(g_ref[r, :], wo_ref[...])
            xo_ref[r, :] = x
        ms = jnp.mean(x * x, axis=-1, keepdims=True)
        xn.append((x * lax.rsqrt(ms + RMS_EPS) * nw_ref[...]).astype(BF16))
    xn = jnp.concatenate(xn, axis=0)
    extra = _dot(xn, we_ref[...])
    if has_cumsum:
        ccol_ref, crow_ref, carry_ref = next(it), next(it), next(it)

        @pl.when(pl.program_id(0) % tiles_per_seq == 0)
        def _():
            carry_ref[...] = jnp.zeros_like(carry_ref)

        log_f = _log_sigmoid(extra + bias_ref[...]) * LOG2_E
        tm = log_f.shape[0]
        cs = _chunk_cumsum(log_f, tm) + carry_ref[...]
        carry_ref[...] = cs[tm - 1:tm, :]
        ccol_ref[...] = cs
        crow_ref[0] = cs.T[:FOX_HEADS, :]
    else:
        next(it)[...] = extra
    n = o_ref.shape[1]
    for j in range(n // tn):
        acc = _dot(xn, w_ref[:, j * tn:(j + 1) * tn])
        for g in range(tn // LANES):
            col = j * tn + g * LANES
            seg = acc[:, g * LANES:(g + 1) * LANES]
            if col // LANES < n_norm_groups:
                msq = jnp.mean(seg * seg, axis=-1, keepdims=True)
                seg = seg * lax.rsqrt(msq + RMS_EPS) * gain_ref[:, col:col + LANES]
            o_ref[:, col:col + LANES] = seg.astype(o_ref.dtype)


def _inproj(x2d, prev, norm_w, w_main, w_ext, gain, n_norm_groups, out_dtype, name, forget_bias=None, seq_len=None):
    t, d = x2d.shape
    n = w_main.shape[1]
    tm, tn = PROJ_ROWS, PROJ_COLS
    has_outproj = prev is not None
    has_cumsum = forget_bias is not None
    row = lambda i: (i, 0)
    fixed = lambda i: (0, 0)
    resident = pl.Buffered(1)
    operands, in_specs = [x2d], [pl.BlockSpec((tm, d), row)]
    if has_outproj:
        g2d, w_out = prev
        operands += [g2d, w_out.astype(BF16)]
        in_specs += [pl.BlockSpec((tm, g2d.shape[1]), row), pl.BlockSpec(w_out.shape, fixed, pipeline_mode=resident)]
    operands += [norm_w.reshape(1, d), w_main, w_ext, gain]
    in_specs += [pl.BlockSpec((1, d), fixed), pl.BlockSpec((d, n), fixed, pipeline_mode=resident),
                 pl.BlockSpec((d, LANES), fixed),
                 pl.BlockSpec((1, gain.shape[1]), fixed)]
    out_shape, out_specs, scratch = [], [], []
    if has_outproj:
        out_shape.append(jax.ShapeDtypeStruct((t, d), F32))
        out_specs.append(pl.BlockSpec((tm, d), row))
    out_shape.append(jax.ShapeDtypeStruct((t, n), out_dtype))
    out_specs.append(pl.BlockSpec((tm, n), row))
    out_shape.append(jax.ShapeDtypeStruct((t, LANES), F32))
    out_specs.append(pl.BlockSpec((tm, LANES), row))
    tiles_per_seq = None
    if has_cumsum:
        tiles_per_seq = seq_len // tm
        operands += [jnp.zeros((1, LANES), F32).at[0, :FOX_HEADS].set(forget_bias)]
        in_specs += [pl.BlockSpec((1, LANES), fixed)]
        out_shape.append(jax.ShapeDtypeStruct((t // seq_len, FOX_HEADS, seq_len), F32))
        out_specs.append(pl.BlockSpec((1, FOX_HEADS, tm), lambda i: (i // tiles_per_seq, 0, i % tiles_per_seq)))
        scratch.append(pltpu.VMEM((1, LANES), F32))
    kern = functools.partial(_inproj_kernel, tn=tn, n_norm_groups=n_norm_groups, has_outproj=has_outproj,
                             has_cumsum=has_cumsum, tiles_per_seq=tiles_per_seq)
    outs = pl.pallas_call(
        kern,
        grid=(t // tm,),
        in_specs=in_specs,
        out_specs=out_specs,
        out_shape=out_shape,
        scratch_shapes=scratch,
        compiler_params=_params("arbitrary" if has_cumsum else "parallel"),
        name=name,
    )(*operands)
    x_out = outs[0] if has_outproj else x2d
    rest = outs[1:] if has_outproj else outs
    return x_out, rest[0], (tuple(rest[1:]) if has_cumsum else rest[1])


def _split_w_in(w_in, n_main):
    d, n = w_in.shape
    w_main = w_in[:, :n_main].astype(BF16)
    w_ext = jnp.zeros((d, LANES), F32).at[:, :n - n_main].set(w_in[:, n_main:]).astype(BF16)
    return w_main, w_ext


def _outproj_kernel(g_hbm, w_ref, x_hbm, o_hbm, *, tm):
    k, d = w_ref.shape

    def tile(g_ref, x_ref, o_ref):
        o_ref[...] = x_ref[...] + _dot(g_ref[...], w_ref[...])

    pltpu.emit_pipeline(
        tile,
        grid=(x_hbm.shape[0] // tm,),
        in_specs=[
            pl.BlockSpec((tm, k), lambda i: (i, 0), pipeline_mode=pl.Buffered(3)),
            pl.BlockSpec((tm, d), lambda i: (i, 0), pipeline_mode=pl.Buffered(3)),
        ],
        out_specs=[pl.BlockSpec((tm, d), lambda i: (i, 0))],
    )(g_hbm, x_hbm, o_hbm)


def _outproj(g2d, w_out, x2d, name):
    t, d = x2d.shape
    return pl.pallas_call(
        functools.partial(_outproj_kernel, tm=PROJ_ROWS),
        in_specs=[
            pl.BlockSpec(memory_space=pl.ANY),
            pl.BlockSpec(memory_space=pltpu.VMEM),
            pl.BlockSpec(memory_space=pl.ANY),
        ],
        out_specs=pl.BlockSpec(memory_space=pl.ANY),
        out_shape=jax.ShapeDtypeStruct((t, d), F32),
        compiler_params=pltpu.CompilerParams(vmem_limit_bytes=VMEM_LIMIT_BYTES),
        name=name,
    )(g2d, w_out.astype(BF16), x2d)


def _fox_attn_kernel(q_ref, k_ref, v_ref, z_ref, ccol_ref, crow_ref, o_ref, *, tq, bq, tk):
    h = pl.program_id(1)
    i = pl.program_id(2)
    nsub = tq // bq
    lane = lax.broadcasted_iota(jnp.int32, (bq, LANES), 1)
    qs = [q_ref[0, r * bq:(r + 1) * bq, :] for r in range(nsub)]
    cqs = [jnp.sum(jnp.where(lane == h, ccol_ref[0, r * bq:(r + 1) * bq, :], 0.0), axis=-1, keepdims=True)
           for r in range(nsub)]
    row = lax.broadcasted_iota(jnp.int32, (bq, tk), 0)
    col = lax.broadcasted_iota(jnp.int32, (bq, tk), 1)

    def load_keys(start):
        return (k_ref[0, pl.ds(start, tk), :], v_ref[0, pl.ds(start, tk), :], crow_ref[0, pl.ds(h, 1), pl.ds(start, tk)])

    def attend(r, carry, keys, row_minus_col):
        m, l, acc = carry
        k, v, ck = keys
        t = _dot_nt(qs[r], k) - ck
        if row_minus_col is not None:
            t = jnp.where(row + row_minus_col >= col, t, -jnp.inf)
        m_new = jnp.maximum(m, jnp.max(t, axis=-1, keepdims=True) + cqs[r])
        alpha = jnp.exp2(m - m_new)
        p = jnp.exp2(t + (cqs[r] - m_new))
        l = alpha * l + jnp.sum(p, axis=-1, keepdims=True)
        acc = alpha * acc + _dot(p.astype(BF16), v)
        return m_new, l, acc

    def below_diagonal(kb, carries):
        keys = load_keys(pl.multiple_of(kb * tk, tk))
        return tuple(attend(r, carries[r], keys, None) for r in range(nsub))

    init = tuple((jnp.full((bq, 1), -jnp.inf, F32), jnp.zeros((bq, 1), F32), jnp.zeros((bq, LANES), F32))
                 for _ in range(nsub))
    carries = list(lax.fori_loop(0, i * (tq // tk), below_diagonal, init))
    for c in range(tq // tk):
        keys = load_keys(pl.multiple_of(i * tq + c * tk, tk))
        for r in range(nsub):
            row0, col0 = r * bq, c * tk
            if col0 > row0 + bq - 1:
                continue
            crosses = col0 + tk - 1 > row0
            carries[r] = attend(r, carries[r], keys, (row0 - col0) if crosses else None)
    for r in range(nsub):
        _, l, acc = carries[r]
        z = z_ref[0, r * bq:(r + 1) * bq, :].astype(F32)
        o_ref[0, r * bq:(r + 1) * bq, :] = (acc / l * _silu(z)).astype(o_ref.dtype)


def _fox_attn_bounded_kernel(q_ref, k_ref, v_ref, z_ref, ccol_ref, crow_ref, bound_ref, o_ref, *, tq, tk, unroll):
    h = pl.program_id(1)
    i = pl.program_id(2)
    nsub = tq // tk
    lane = lax.broadcasted_iota(jnp.int32, (tq, LANES), 1)
    cq = jnp.sum(jnp.where(lane == h, ccol_ref[0], 0.0), axis=-1, keepdims=True)
    dq = cq - bound_ref[0:1, 0:1]
    q = q_ref[0]
    ones = jnp.ones((tk, LANES), BF16)

    def weighted_values(start, first_row, diagonal):
        k = k_ref[0, pl.ds(start, tk), :]
        v1 = jnp.concatenate([v_ref[0, pl.ds(start, tk), :], ones], axis=1)
        ck = crow_ref[0, pl.ds(h, 1), pl.ds(start, tk)]
        e = (_dot_nt(q[first_row:], k) - ck) + dq[first_row:]
        if diagonal:
            row = lax.broadcasted_iota(jnp.int32, (tk, tk), 0)
            col = lax.broadcasted_iota(jnp.int32, (tk, tk), 1)
            top = jnp.where(row >= col, e[:tk], -jnp.inf)
            e = top if e.shape[0] == tk else jnp.concatenate([top, e[tk:]], axis=0)
        return _dot(jnp.exp2(e).astype(BF16), v1)

    def below_diagonal(jj, acc):
        for u in range(unroll):
            acc = acc + weighted_values(pl.multiple_of((jj * unroll + u) * tk, tk), 0, False)
        return acc

    acc = lax.fori_loop(0, i * nsub // unroll, below_diagonal, jnp.zeros((tq, 2 * LANES), F32))
    for c in range(nsub):
        pv = weighted_values(pl.multiple_of(i * tq + c * tk, tk), c * tk, True)
        acc = acc + pv if c == 0 else jnp.concatenate([acc[:c * tk], acc[c * tk:] + pv], axis=0)
    z = z_ref[0].astype(F32)
    o_ref[0] = (acc[:, :LANES] / acc[:, LANES:LANES + 1] * _silu(z)).astype(o_ref.dtype)


def _fox_attn(y, ccol, crow, logit_bound):
    b, s, _ = y.shape
    h = FOX_HEADS

    def specs(tq):
        return [
            pl.BlockSpec((1, tq, LANES), lambda bi, hi, i: (bi, i, hi)),
            pl.BlockSpec((1, s, LANES), lambda bi, hi, i: (bi, 0, h + hi)),
            pl.BlockSpec((1, s, LANES), lambda bi, hi, i: (bi, 0, 2 * h + hi)),
            pl.BlockSpec((1, tq, LANES), lambda bi, hi, i: (bi, i, 3 * h + hi)),
            pl.BlockSpec((1, tq, LANES), lambda bi, hi, i: (bi, i, 0)),
            pl.BlockSpec((1, h, s), lambda bi, hi, i: (bi, 0, 0)),
        ]

    def online():
        tq, bq, tk = min(FOX_TQ, s), FOX_BQ, FOX_TK
        return pl.pallas_call(
            functools.partial(_fox_attn_kernel, tq=tq, bq=bq, tk=tk),
            grid=(b, h, s // tq),
            in_specs=specs(tq),
            out_specs=pl.BlockSpec((1, tq, LANES), lambda bi, hi, i: (bi, i, hi)),
            out_shape=jax.ShapeDtypeStruct((b, s, FOX_WIDTH), BF16),
            compiler_params=_params("parallel", "parallel", "arbitrary"),
            name="fox_attn",
        )(y, y, y, y, ccol, crow)

    def bounded():
        tq, tk = min(FOX_BOUNDED_TQ, s), FOX_TK
        unroll = tq // tk
        return pl.pallas_call(
            functools.partial(_fox_attn_bounded_kernel, tq=tq, tk=tk, unroll=unroll),
            grid=(b, h, s // tq),
            in_specs=specs(tq) + [pl.BlockSpec((1, LANES), lambda bi, hi, i: (0, 0))],
            out_specs=pl.BlockSpec((1, tq, LANES), lambda bi, hi, i: (bi, i, hi)),
            out_shape=jax.ShapeDtypeStruct((b, s, FOX_WIDTH), BF16),
            compiler_params=_params("parallel", "parallel", "arbitrary"),
            name="fox_attn_bounded",
        )(y, y, y, y, ccol, crow, jnp.full((1, LANES), logit_bound, F32))

    return lax.cond(logit_bound <= FOX_MAX_FIXED_SHIFT, bounded, online)


def _fox_layer(x2d, prev, bsz, norm_w, w_in, b_f, q_gain, k_gain, tag):
    t, d = x2d.shape
    s = t // bsz
    w_main, w_ext = _split_w_in(w_in, 4 * FOX_WIDTH)
    scale = FOX_HEAD_DIM ** -0.5 * LOG2_E
    gain = jnp.concatenate([jnp.tile(q_gain * scale, FOX_HEADS), jnp.tile(k_gain, FOX_HEADS)]).reshape(1, -1)
    x2d, y, (ccol, crow) = _inproj(x2d, prev, norm_w, w_main, w_ext, gain, 2 * FOX_HEADS, BF16, "fox_front" + tag,
                                   forget_bias=b_f, seq_len=s)
    logit_bound = FOX_HEAD_DIM * jnp.max(jnp.abs(q_gain * scale)) * jnp.max(jnp.abs(k_gain)) * 1.02
    g = _fox_attn(y.reshape(bsz, s, -1), ccol.reshape(bsz, s, LANES), crow, logit_bound)
    return x2d, g.reshape(t, FOX_WIDTH)


def _gla_kernel(q_ref, k_ref, v_ref, z_ref, ext_ref, wup_ref, bg_ref, gain_ref, o_ref, state_ref, *, rows):
    chunk = GLA_CHUNK
    chunks = [slice(c * chunk, (c + 1) * chunk) for c in range(rows // chunk)]

    @pl.when(pl.program_id(2) == 0)
    def _():
        state_ref[...] = jnp.zeros_like(state_ref)

    gate = _dot(ext_ref[0].astype(BF16), wup_ref[...]) + bg_ref[...]
    log_a = _log_sigmoid(gate) * (1.0 / GLA_TAU)
    bcum = _chunk_cumsum(log_a, chunk)
    q = q_ref[0].astype(F32) * (GLA_KEY_DIM ** -0.5)
    k = k_ref[0].astype(F32)
    q_dec = (q * jnp.exp(bcum)).astype(BF16)
    k_inv = (k * jnp.exp(-bcum)).astype(BF16)
    row = lax.broadcasted_iota(jnp.int32, (chunk, chunk), 0)
    col = lax.broadcasted_iota(jnp.int32, (chunk, chunk), 1)
    causal = row >= col
    b_last = [bcum[sl][chunk - 1:chunk, :] for sl in chunks]
    k_dec = [(k[sl] * jnp.exp(b_last[c] - bcum[sl])).astype(BF16) for c, sl in enumerate(chunks)]
    v16 = [v_ref[0, sl, :] for sl in chunks]
    attn = [jnp.where(causal, _dot_nt(q_dec[sl], k_inv[sl]), 0.0).astype(BF16) for sl in chunks]
    update = [_dot(v16[c].astype(F32).T.astype(BF16), k_dec[c]) for c in range(len(chunks))]
    state_t = state_ref[...]
    s16 = []
    for c in range(len(chunks)):
        s16.append(state_t.astype(BF16))
        state_t = state_t * jnp.exp(b_last[c]) + update[c]
    state_ref[...] = state_t
    for c, sl in enumerate(chunks):
        o = _dot(attn[c], v16[c]) + _dot_nt(q_dec[sl], s16[c])
        ms = jnp.mean(o * o, axis=-1, keepdims=True)
        on = o * lax.rsqrt(ms + RMS_EPS) * gain_ref[...]
        o_ref[0, sl, :] = (on * _silu(z_ref[0, sl, :].astype(F32))).astype(o_ref.dtype)


def _gla_layer(x2d, prev, bsz, norm_w, w_in, w_gate_up, b_gate, o_gain):
    t, d = x2d.shape
    s = t // bsz
    h, dk, dv = GLA_HEADS, GLA_KEY_DIM, GLA_VAL_DIM
    n_main = 2 * h * dk + 2 * GLA_WIDTH
    w_main, w_ext = _split_w_in(w_in, n_main)
    x2d, y, ext = _inproj(x2d, prev, norm_w, w_main, w_ext, jnp.ones((1, LANES), F32), 0, BF16, "gla_front")
    y = y.reshape(bsz, s, n_main)
    ext = ext.reshape(bsz, s, LANES)
    wup = jnp.zeros((LANES, h * dk), F32).at[:GLA_RANK].set(w_gate_up).astype(BF16)
    rows = min(GLA_ROWS, s)
    kern = functools.partial(_gla_kernel, rows=rows)
    k_off = h * dk // dk
    v_off = 2 * h * dk // dv
    z_off = (2 * h * dk + GLA_WIDTH) // dv
    g = pl.pallas_call(
        kern,
        grid=(bsz, h, s // rows),
        in_specs=[
            pl.BlockSpec((1, rows, dk), lambda bi, hi, i: (bi, i, hi)),
            pl.BlockSpec((1, rows, dk), lambda bi, hi, i: (bi, i, k_off + hi)),
            pl.BlockSpec((1, rows, dv), lambda bi, hi, i: (bi, i, v_off + hi)),
            pl.BlockSpec((1, rows, dv), lambda bi, hi, i: (bi, i, z_off + hi)),
            pl.BlockSpec((1, rows, LANES), lambda bi, hi, i: (bi, i, 0)),
            pl.BlockSpec((LANES, dk), lambda bi, hi, i: (0, hi)),
            pl.BlockSpec((1, dk), lambda bi, hi, i: (0, hi)),
            pl.BlockSpec((1, dv), lambda bi, hi, i: (0, 0)),
        ],
        out_specs=pl.BlockSpec((1, rows, dv), lambda bi, hi, i: (bi, i, hi)),
        out_shape=jax.ShapeDtypeStruct((bsz, s, GLA_WIDTH), BF16),
        scratch_shapes=[pltpu.VMEM((dv, dk), F32)],
        compiler_params=_params("parallel", "parallel", "arbitrary"),
        name="gla_chunk",
    )(y, y, y, y, ext, wup, b_gate.reshape(1, -1), o_gain.reshape(1, -1))
    return x2d, g.reshape(t, GLA_WIDTH)


GDN_HISTORY = 8


def _gdn_conv_heads(u_ref, w_ref, hist_ref, rows):
    pad = GDN_HISTORY
    first = pl.program_id(1) == 0

    @pl.when(first)
    def _():
        hist_ref[0:pad, :] = jnp.zeros((pad, hist_ref.shape[1]), F32)

    @pl.when(jnp.logical_not(first))
    def _():
        hist_ref[0:pad, :] = hist_ref[rows:rows + pad, :]

    hist_ref[pad:pad + rows, :] = u_ref[0].astype(F32)
    heads = []
    for g in range(GDN_CONV_DIM // LANES):
        cols = slice(g * LANES, (g + 1) * LANES)
        acc = hist_ref[pad:pad + rows, cols] * w_ref[GDN_CONV - 1:GDN_CONV, cols]
        for tap in range(GDN_CONV - 1):
            off = pad - (GDN_CONV - 1) + tap
            acc = acc + hist_ref[off:off + rows, cols] * w_ref[tap:tap + 1, cols]
        seg = _silu(acc)
        if g < 2 * GDN_QK_HEADS:
            ss = jnp.sum(seg * seg, axis=-1, keepdims=True)
            seg = seg * lax.rsqrt(ss + RMS_EPS)
            if g < GDN_QK_HEADS:
                seg = seg * (GDN_HEAD_DIM ** -0.5)
        heads.append(seg)
    n_hq = GDN_QK_HEADS
    return heads[:n_hq], heads[n_hq:2 * n_hq], heads[2 * n_hq:]


def _gdn_kernel(u_ref, z_ref, ext_ref, cw_ref, alog_ref, dtb_ref, gain_ref, o_ref, state_ref, hist_ref, *, rows):
    chunk = GDN_CHUNK
    d = GDN_HEAD_DIM
    n_hq, n_hv = GDN_QK_HEADS, GDN_V_HEADS
    rep = n_hv // n_hq
    chunks = [slice(c * chunk, (c + 1) * chunk) for c in range(rows // chunk)]
    pairs = [(hv, c) for hv in range(n_hv) for c in range(len(chunks))]

    @pl.when(pl.program_id(1) == 0)
    def _():
        state_ref[...] = jnp.zeros_like(state_ref)

    q, k, v_heads = _gdn_conv_heads(u_ref, cw_ref, hist_ref, rows)

    ext = ext_ref[0]
    g_all = -jnp.exp(alog_ref[...]) * _softplus(ext + dtb_ref[...])
    gc_all = _chunk_cumsum(g_all, chunk)
    gc_all_t = gc_all.T
    beta_all = _sigmoid(ext)
    row = lax.broadcasted_iota(jnp.int32, (chunk, chunk), 0)
    col = lax.broadcasted_iota(jnp.int32, (chunk, chunk), 1)
    causal = row >= col
    strict = row > col

    q16 = [x.astype(BF16) for x in q]
    k16 = [x.astype(BF16) for x in k]
    kk = {(h, c): _dot_nt(k16[h][sl], k16[h][sl]) for h in range(n_hq) for c, sl in enumerate(chunks)}
    qk = {(h, c): _dot_nt(q16[h][sl], k16[h][sl]) for h in range(n_hq) for c, sl in enumerate(chunks)}

    gc, beta_b, rhs, q_dec = [], [], [], []
    for hv in range(n_hv):
        h = hv // rep
        gc_h = jnp.broadcast_to(gc_all[:, hv:hv + 1], (rows, LANES))
        beta = jnp.broadcast_to(beta_all[:, n_hv + hv:n_hv + hv + 1], (rows, LANES))
        eg = jnp.exp(gc_h)
        v = v_heads[hv]
        gc.append(gc_h)
        beta_b.append(beta)
        rhs.append(jnp.concatenate([v * beta, k[h] * (beta * eg)], axis=-1))
        q_dec.append((q[h] * eg).astype(BF16))

    decay, nk, m = {}, {}, {}
    for hv, c in pairs:
        sl = chunks[c]
        diff = gc[hv][sl, :chunk] - gc_all_t[hv:hv + 1, sl]
        decay[hv, c] = jnp.where(causal, jnp.exp(jnp.where(causal, diff, 0.0)), 0.0)
        beta = beta_b[hv][sl, :chunk]
        nk[hv, c] = -jnp.where(strict, kk[hv // rep, c] * beta * decay[hv, c], 0.0)
        m[hv, c] = nk[hv, c]
    span = 2
    while span < chunk:
        n16 = {p: nk[p].astype(BF16) for p in pairs}
        nk = {p: _dot(n16[p], n16[p]) for p in pairs}
        mn = {p: _dot(m[p].astype(BF16), nk[p].astype(BF16)) for p in pairs}
        m = {p: m[p] + nk[p] + mn[p] for p in pairs}
        span *= 2
    uw = {(hv, c): rhs[hv][chunks[c]] + _dot(m[hv, c].astype(BF16), rhs[hv][chunks[c]].astype(BF16))
          for hv, c in pairs}

    heads = range(n_hv)
    state = [state_ref[hv] for hv in heads]
    for c, sl in enumerate(chunks):
        g_last = [gc[hv][sl][chunk - 1:chunk, :] for hv in heads]
        k_dec_t = [(k[hv // rep][sl] * jnp.exp(g_last[hv] - gc[hv][sl])).T.astype(BF16) for hv in heads]
        qk16 = [(qk[hv // rep, c] * decay[hv, c]).astype(BF16) for hv in heads]
        s16 = [state[hv].astype(BF16) for hv in heads]
        ws = [_dot(uw[hv, c][:, d:].astype(BF16), s16[hv]) for hv in heads]
        qs = [_dot(q_dec[hv][sl], s16[hv]) for hv in heads]
        v16 = [(uw[hv, c][:, :d] - ws[hv]).astype(BF16) for hv in heads]
        o = [qs[hv] + _dot(qk16[hv], v16[hv]) for hv in heads]
        state = [state[hv] * jnp.exp(g_last[hv]) + _dot(k_dec_t[hv], v16[hv]) for hv in heads]
        for hv in heads:
            ms = jnp.mean(o[hv] * o[hv], axis=-1, keepdims=True)
            on = o[hv] * lax.rsqrt(ms + RMS_EPS) * gain_ref[...]
            z = z_ref[0, sl, hv * d:(hv + 1) * d].astype(F32)
            o_ref[0, sl, hv * d:(hv + 1) * d] = (on * _silu(z)).astype(o_ref.dtype)
    for hv in heads:
        state_ref[hv] = state[hv]


def _gdn_layer(x2d, prev, bsz, norm_w, w_in, conv_w, a_log, dt_bias, o_gain):
    t, dm = x2d.shape
    s = t // bsz
    hv, d = GDN_V_HEADS, GDN_HEAD_DIM
    n_main = GDN_CONV_DIM + GDN_WIDTH
    w_main, w_ext = _split_w_in(w_in, n_main)
    x2d, y, ext = _inproj(x2d, prev, norm_w, w_main, w_ext, jnp.ones((1, LANES), F32), 0, BF16, "gdn_front")
    y = y.reshape(bsz, s, n_main)
    ext = ext.reshape(bsz, s, LANES)
    alog = jnp.zeros((1, LANES), F32).at[0, :hv].set(a_log)
    dtb = jnp.zeros((1, LANES), F32).at[0, :hv].set(dt_bias)
    rows = min(GDN_ROWS, s)
    kern = functools.partial(_gdn_kernel, rows=rows)
    g = pl.pallas_call(
        kern,
        grid=(bsz, s // rows),
        in_specs=[
            pl.BlockSpec((1, rows, GDN_CONV_DIM), lambda bi, i: (bi, i, 0)),
            pl.BlockSpec((1, rows, GDN_WIDTH), lambda bi, i: (bi, i, GDN_CONV_DIM // GDN_WIDTH)),
            pl.BlockSpec((1, rows, LANES), lambda bi, i: (bi, i, 0)),
            pl.BlockSpec((GDN_CONV, GDN_CONV_DIM), lambda bi, i: (0, 0)),
            pl.BlockSpec((1, LANES), lambda bi, i: (0, 0)),
            pl.BlockSpec((1, LANES), lambda bi, i: (0, 0)),
            pl.BlockSpec((1, d), lambda bi, i: (0, 0)),
        ],
        out_specs=pl.BlockSpec((1, rows, GDN_WIDTH), lambda bi, i: (bi, i, 0)),
        out_shape=jax.ShapeDtypeStruct((bsz, s, GDN_WIDTH), BF16),
        scratch_shapes=[pltpu.VMEM((hv, d, d), F32), pltpu.VMEM((GDN_HISTORY + rows, GDN_CONV_DIM), F32)],
        compiler_params=_params("parallel", "arbitrary"),
        name="gdn_chunk",
    )(y, y, ext, conv_w, alog, dtb, o_gain.reshape(1, -1))
    return x2d, g.reshape(t, GDN_WIDTH)


def kernel(x, norm_w, fox_w_in, fox_b_f, fox_q_gain, fox_k_gain, fox_w_out, gla_w_in, gla_w_gate_up, gla_b_gate, gla_o_gain, gla_w_out, gdn_w_in, gdn_conv_w, gdn_a_log, gdn_dt_bias, gdn_o_gain, gdn_w_out):
    bsz, s, d = x.shape
    x2d = x.reshape(bsz * s, d)
    prev = None
    for layer in range(DEPTH):
        kind, idx = layer % N_MIXERS, layer // N_MIXERS
        if kind == 0:
            x2d, g = _fox_layer(x2d, prev, bsz, norm_w[layer], fox_w_in[idx], fox_b_f[idx], fox_q_gain[idx],
                                fox_k_gain[idx], str(idx))
            prev = (g, fox_w_out[idx])
        elif kind == 1:
            x2d, g = _gla_layer(x2d, prev, bsz, norm_w[layer], gla_w_in[idx], gla_w_gate_up[idx], gla_b_gate[idx],
                                gla_o_gain[idx])
            prev = (g, gla_w_out[idx])
        else:
            x2d, g = _gdn_layer(x2d, prev, bsz, norm_w[layer], gdn_w_in[idx], gdn_conv_w[idx], gdn_a_log[idx],
                                gdn_dt_bias[idx], gdn_o_gain[idx])
            prev = (g, gdn_w_out[idx])
    x2d = _outproj(prev[0], prev[1], x2d, "outproj_last")
    return x2d.reshape(bsz, s, d)
```
